```python
import math
import jax
import jax.numpy as jnp
from jax import lax
import numpy as np

D_MODEL = 4096
BATCH = 1
SEQ = 8192
DEPTH = 2
DEC_BATCH = 16
DEC_SEQ = 16
PAST_LEN = 4096

CHUNK = 64
N_MIXERS = 2
S5_GROUP_DIM = 16
S5_GROUPS = D_MODEL // S5_GROUP_DIM
S5_STATE = 64
DT_MIN = 1e-3
DT_MAX = 1e-1
N_HEADS = 32
HEAD_DIM = D_MODEL // N_HEADS
FOX_PROJ = 3 * D_MODEL + N_HEADS
FORGET_W_SCALE = 0.1
Q_BLOCK = 128
D_FF = 7 * D_MODEL // 2
N_EXPERTS = 8
TOP_K = 2
RMS_EPS = 1e-5
NEG_INF = -1e30

kernel_name = "hybrid_s5_fox_streaming_step"


def rmsnorm(x, g):
    xf = x.astype(jnp.float32)
    y = xf * lax.rsqrt(jnp.mean(xf * xf, axis=-1, keepdims=True) + RMS_EPS)
    return (y * g.astype(jnp.float32)).astype(x.dtype)


def swiglu(x, w_gate, w_up, w_down):
    return (jax.nn.silu(x @ w_gate) * (x @ w_up)) @ w_down


def _ssm_combine(left, right):
    a_l, b_l = left
    a_r, b_r = right
    return a_r * a_l, a_r * b_l + b_r


def s5_mixer(u, h0_re, h0_im, prm):
    f32 = jnp.float32
    bsz, t_len, _ = u.shape
    uf = u.astype(f32)
    lam = lax.complex(prm["s5_a_re"].astype(f32), prm["s5_a_im"].astype(f32))
    dt = jnp.exp(prm["s5_log_dt"].astype(f32))[:, None]
    lam_bar = jnp.exp(lam * dt)
    b = lax.complex(prm["s5_b_re"].astype(f32), prm["s5_b_im"].astype(f32))
    b_bar = ((lam_bar - 1.0) / lam)[..., None] * b
    ug = uf.reshape(bsz, t_len, S5_GROUPS, S5_GROUP_DIM).astype(jnp.complex64)
    bu = jnp.einsum("gpc,btgc->btgp", b_bar, ug)
    h0 = lax.complex(h0_re.astype(f32), h0_im.astype(f32))
    bu = bu.at[:, 0].add(lam_bar * h0)
    a = jnp.broadcast_to(lam_bar, bu.shape)
    _, h = lax.associative_scan(_ssm_combine, (a, bu), axis=1)
    y = (jnp.einsum("gcp,btgp->btgc", prm["s5_c_re"].astype(f32), h.real)
         - jnp.einsum("gcp,btgp->btgc", prm["s5_c_im"].astype(f32), h.imag))
    y = y.reshape(bsz, t_len, D_MODEL) + prm["s5_d"].astype(f32) * uf
    z = jax.nn.gelu(y)
    out = (z @ prm["s5_w_glu_a"].astype(f32)) * jax.nn.sigmoid(z @ prm["s5_w_glu_b"].astype(f32))
    h_last = h[:, -1]
    return out.astype(u.dtype), h_last.real, h_last.imag


def fox_qkvf(u, w_in, b_f):
    bsz, t_len, _ = u.shape
    proj = u @ w_in
    q = proj[..., 0 * D_MODEL:1 * D_MODEL].reshape(bsz, t_len, N_HEADS, HEAD_DIM)
    k = proj[..., 1 * D_MODEL:2 * D_MODEL].reshape(bsz, t_len, N_HEADS, HEAD_DIM)
    v = proj[..., 2 * D_MODEL:3 * D_MODEL].reshape(bsz, t_len, N_HEADS, HEAD_DIM)
    logf = jax.nn.log_sigmoid(proj[..., 3 * D_MODEL:].astype(jnp.float32) + b_f.astype(jnp.float32))
    return q, k, v, logf


def fox_prompt_attention(q, k, v, logf):
    bsz, t_len = q.shape[:2]
    nb = t_len // Q_BLOCK
    scale = HEAD_DIM ** -0.5
    c = jnp.cumsum(logf, axis=1)
    c_k = c.transpose(0, 2, 1)
    k_pos = jnp.arange(t_len)
    qb = q.reshape(bsz, nb, Q_BLOCK, N_HEADS, HEAD_DIM).swapaxes(0, 1)
    cb = c.reshape(bsz, nb, Q_BLOCK, N_HEADS).swapaxes(0, 1)

    def block(args):
        q_i, c_i, i = args
        s = jnp.einsum("bqhd,bkhd->bhqk", q_i, k).astype(jnp.float32) * scale
        bias = c_i.transpose(0, 2, 1)[..., None] - c_k[:, :, None, :]
        q_pos = i * Q_BLOCK + jnp.arange(Q_BLOCK)
        mask = k_pos[None, :] <= q_pos[:, None]
        p = jax.nn.softmax(jnp.where(mask, s + bias, NEG_INF), axis=-1)
        return jnp.einsum("bhqk,bkhd->bqhd", p.astype(v.dtype), v)

    o = lax.map(block, (qb, cb, jnp.arange(nb)))
    return o.swapaxes(0, 1).reshape(bsz, t_len, D_MODEL)


def fox_sample_attention(q, k, v, logf, k_cache, v_cache, logf_cache):
    bsz, t_len = q.shape[:2]
    past = k_cache.shape[1]
    scale = HEAD_DIM ** -0.5
    c_past = jnp.cumsum(logf_cache.astype(jnp.float32), axis=1)
    c_new = c_past[:, -1:, :] + jnp.cumsum(logf, axis=1)
    cq = c_new.transpose(0, 2, 1)[..., None]
    s_past = (jnp.einsum("bqhd,bkhd->bhqk", q, k_cache).astype(jnp.float32) * scale
              + cq - c_past.transpose(0, 2, 1)[:, :, None, :])
    s_new = (jnp.einsum("bqhd,bkhd->bhqk", q, k).astype(jnp.float32) * scale
             + cq - c_new.transpose(0, 2, 1)[:, :, None, :])
    causal = jnp.tril(jnp.ones((t_len, t_len), dtype=bool))
    s_new = jnp.where(causal, s_new, NEG_INF)
    p = jax.nn.softmax(jnp.concatenate([s_past, s_new], axis=-1), axis=-1)
    o = (jnp.einsum("bhqk,bkhd->bqhd", p[..., :past].astype(v.dtype), v_cache)
         + jnp.einsum("bhqk,bkhd->bqhd", p[..., past:].astype(v.dtype), v))
    return o.reshape(bsz, t_len, D_MODEL)


def moe_swiglu(u, router, w_gate, w_up, w_down):
    logits = (u @ router).astype(jnp.float32)
    top_v, top_i = lax.top_k(logits, TOP_K)
    gates = jax.nn.softmax(top_v, axis=-1)
    combine = jnp.sum(jax.nn.one_hot(top_i, N_EXPERTS, dtype=jnp.float32) * gates[..., None], axis=-2)
    out = jnp.zeros_like(u)
    for e in range(N_EXPERTS):
        out = out + combine[..., e:e + 1].astype(u.dtype) * swiglu(u, w_gate[e], w_up[e], w_down[e])
    return out


def trunk(x, s5_h0_re, s5_h0_im, fox_past, prm):
    for i in range(DEPTH):
        if i % N_MIXERS == 0:
            y, h_re, h_im = s5_mixer(rmsnorm(x, prm["norm_mix0"]), s5_h0_re, s5_h0_im, prm)
            x = x + y
            x = x + swiglu(rmsnorm(x, prm["norm_ffn0"]), prm["ffn_w_gate"], prm["ffn_w_up"], prm["ffn_w_down"])
        else:
            q, k, v, logf = fox_qkvf(rmsnorm(x, prm["norm_mix1"]), prm["fox_w_in"], prm["fox_b_f"])
            if fox_past is None:
                o = fox_prompt_attention(q, k, v, logf)
            else:
                o = fox_sample_attention(q, k, v, logf, fox_past[0], fox_past[1], fox_past[2])
            x = x + o @ prm["fox_w_o"]
            x = x + moe_swiglu(rmsnorm(x, prm["norm_ffn1"]), prm["moe_router"], prm["moe_w_gate"],
                               prm["moe_w_up"], prm["moe_w_down"])
    return rmsnorm(x, prm["norm_final"]), h_re, h_im, k, v, logf


def setup_inputs(seed: int = 0) -> dict:
    key = jax.random.key(seed)
    ks = jax.random.split(key, 32)
    f32 = jnp.float32

    def nrm(k, shape, scale=1.0):
        return jax.random.normal(k, shape, f32) * scale

    def gain(k):
        return 1.0 + 0.02 * jax.random.normal(k, (D_MODEL,), f32)

    fox_col_scale = jnp.concatenate([jnp.ones((3 * D_MODEL,), f32),
                                     jnp.full((N_HEADS,), FORGET_W_SCALE, f32)]) * D_MODEL ** -0.5
    return {
        "x_prompt": nrm(ks[0], (BATCH, SEQ, D_MODEL)),
        "x_sample": nrm(ks[1], (DEC_BATCH, DEC_SEQ, D_MODEL)),
        "state_s5_re": nrm(ks[2], (DEC_BATCH, S5_GROUPS, S5_STATE), 0.3),
        "state_s5_im": nrm(ks[3], (DEC_BATCH, S5_GROUPS, S5_STATE), 0.3),
        "cache_fox_k": nrm(ks[4], (DEC_BATCH, PAST_LEN, N_HEADS, HEAD_DIM)),
        "cache_fox_v": nrm(ks[5], (DEC_BATCH, PAST_LEN, N_HEADS, HEAD_DIM)),
        "cache_fox_logf": jax.nn.log_sigmoid(3.0 + nrm(ks[6], (DEC_BATCH, PAST_LEN, N_HEADS))),
        "norm_mix0": gain(ks[7]),
        "s5_a_re": -0.5 + nrm(ks[8], (S5_GROUPS, S5_STATE), 0.01),
        "s5_a_im": math.pi * jnp.arange(S5_STATE, dtype=f32)[None, :] + nrm(ks[9], (S5_GROUPS, S5_STATE), 0.01),
        "s5_log_dt": jax.random.uniform(ks[10], (S5_GROUPS,), f32, math.log(DT_MIN), math.log(DT_MAX)),
        "s5_b_re": nrm(ks[11], (S5_GROUPS, S5_STATE, S5_GROUP_DIM), (2 * S5_GROUP_DIM) ** -0.5),
        "s5_b_im": nrm(ks[12], (S5_GROUPS, S5_STATE, S5_GROUP_DIM), (2 * S5_GROUP_DIM) ** -0.5),
        "s5_c_re": nrm(ks[13], (S5_GROUPS, S5_GROUP_DIM, S5_STATE), S5_STATE ** -0.5),
        "s5_c_im": nrm(ks[14], (S5_GROUPS, S5_GROUP_DIM, S5_STATE), S5_STATE ** -0.5),
        "s5_d": nrm(ks[15], (D_MODEL,), 0.5),
        "s5_w_glu_a": nrm(ks[16], (D_MODEL, D_MODEL), D_MODEL ** -0.5),
        "s5_w_glu_b": nrm(ks[17], (D_MODEL, D_MODEL), D_MODEL ** -0.5),
        "norm_ffn0": gain(ks[18]),
        "ffn_w_gate": nrm(ks[19], (D_MODEL, D_FF), D_MODEL ** -0.5),
        "ffn_w_up": nrm(ks[20], (D_MODEL, D_FF), D_MODEL ** -0.5),
        "ffn_w_down": nrm(ks[21], (D_FF, D_MODEL), D_FF ** -0.5),
        "norm_mix1": gain(ks[22]),
        "fox_w_in": nrm(ks[23], (D_MODEL, FOX_PROJ)) * fox_col_scale,
        "fox_b_f": 1.0 + 5.0 * jax.random.uniform(ks[24], (N_HEADS,), f32),
        "fox_w_o": nrm(ks[25], (D_MODEL, D_MODEL), D_MODEL ** -0.5),
        "norm_ffn1": gain(ks[26]),
        "moe_router": nrm(ks[27], (D_MODEL, N_EXPERTS), D_MODEL ** -0.5),
        "moe_w_gate": nrm(ks[28], (N_EXPERTS, D_MODEL, D_FF), D_MODEL ** -0.5),
        "moe_w_up": nrm(ks[29], (N_EXPERTS, D_MODEL, D_FF), D_MODEL ** -0.5),
        "moe_w_down": nrm(ks[30], (N_EXPERTS, D_FF, D_MODEL), D_FF ** -0.5),
        "norm_final": gain(ks[31]),
    }


def reference(x_prompt, x_sample, state_s5_re, state_s5_im, cache_fox_k, cache_fox_v, cache_fox_logf,
              norm_mix0, s5_a_re, s5_a_im, s5_log_dt, s5_b_re, s5_b_im, s5_c_re, s5_c_im, s5_d,
              s5_w_glu_a, s5_w_glu_b, norm_ffn0, ffn_w_gate, ffn_w_up, ffn_w_down,
              norm_mix1, fox_w_in, fox_b_f, fox_w_o, norm_ffn1, moe_router, moe_w_gate, moe_w_up,
              moe_w_down, norm_final):
    prm = {
        "norm_mix0": norm_mix0, "s5_a_re": s5_a_re, "s5_a_im": s5_a_im, "s5_log_dt": s5_log_dt,
        "s5_b_re": s5_b_re, "s5_b_im": s5_b_im, "s5_c_re": s5_c_re, "s5_c_im": s5_c_im, "s5_d": s5_d,
        "s5_w_glu_a": s5_w_glu_a, "s5_w_glu_b": s5_w_glu_b, "norm_ffn0": norm_ffn0,
        "ffn_w_gate": ffn_w_gate, "ffn_w_up": ffn_w_up, "ffn_w_down": ffn_w_down,
        "norm_mix1": norm_mix1, "fox_w_in": fox_w_in, "fox_b_f": fox_b_f, "fox_w_o": fox_w_o,
        "norm_ffn1": norm_ffn1, "moe_router": moe_router, "moe_w_gate": moe_w_gate,
        "moe_w_up": moe_w_up, "moe_w_down": moe_w_down, "norm_final": norm_final,
    }
    h0 = jnp.zeros((x_prompt.shape[0], S5_GROUPS, S5_STATE), jnp.float32)
    y_prompt, s5_re_p, s5_im_p, k_p, v_p, logf_p = trunk(x_prompt, h0, h0, None, prm)
    y_sample, s5_re_s, s5_im_s, k_s, v_s, logf_s = trunk(
        x_sample, state_s5_re, state_s5_im, (cache_fox_k, cache_fox_v, cache_fox_logf), prm)
    return (y_prompt, y_sample, s5_re_p, s5_im_p, s5_re_s, s5_im_s, k_p, v_p, logf_p, k_s, v_s, logf_s)
```

```python
import functools
import math

import jax
import jax.numpy as jnp
from jax import lax
from jax.experimental import pallas as pl
from jax.experimental.pallas import tpu as pltpu

F32 = jnp.float32
BF16 = jnp.bfloat16

RMS_EPS = 1e-5
NEG_INF = -1e30
S5_GROUP_DIM = 16
S5_STATE = 64
S5_BLOCK = 16
HEAD_DIM = 128
LANES = 128
V7X_VMEM_BYTES = 64 << 20


def _cparams(n_grid, vmem_mb):
    return pltpu.CompilerParams(dimension_semantics=("arbitrary",) * n_grid,
                                vmem_limit_bytes=min(vmem_mb << 20, V7X_VMEM_BYTES - (2 << 20)))


def _pick_tile(n, cap, mult):
    for t in range(min(cap, n), 0, -1):
        if n % t == 0 and t % mult == 0:
            return t
    raise ValueError(f"no tile for {n} (cap {cap}, multiple of {mult})")


def _split_bf16(x):
    hi = x.astype(BF16)
    lo = (x - hi.astype(F32)).astype(BF16)
    return hi, lo


def _dot(a, b):
    return jnp.dot(a, b, preferred_element_type=F32)


def _dot3(a_hi, a_lo, b_hi, b_lo):
    return _dot(a_hi, b_hi) + _dot(a_lo, b_hi) + _dot(a_hi, b_lo)


def _rms(x, g):
    y = x * lax.rsqrt(jnp.mean(x * x, axis=-1, keepdims=True) + RMS_EPS)
    return y * g


def _norm_body(*refs, n_add, out_sum, n_out):
    x = refs[0][...]
    for r in refs[1:1 + n_add]:
        x = x + r[...]
    g_ref = refs[1 + n_add]
    outs = refs[2 + n_add:]
    if out_sum:
        outs[0][...] = x
        outs = outs[1:]
    y = _rms(x, g_ref[...])
    for o in outs:
        o[...] = y.astype(o.dtype)


def rmsnorm(x, adds, g, out_dtypes, out_sum=False):
    t, d = x.shape
    tm = _pick_tile(t, 256, 8)
    row = pl.BlockSpec((tm, d), lambda i: (i, 0))
    shapes = ([jax.ShapeDtypeStruct((t, d), F32)] if out_sum else []) + [
        jax.ShapeDtypeStruct((t, d), dt) for dt in out_dtypes]
    return pl.pallas_call(
        functools.partial(_norm_body, n_add=len(adds), out_sum=out_sum, n_out=len(out_dtypes)),
        grid=(t // tm,),
        in_specs=[row] * (1 + len(adds)) + [pl.BlockSpec((1, d), lambda i: (0, 0))],
        out_specs=[row] * len(shapes),
        out_shape=shapes,
        compiler_params=_cparams(1, 48),
        name="rmsnorm",
    )(x, *adds, g.reshape(1, d))


def _log_sigmoid(x):
    return jnp.minimum(x, 0.0) - jnp.log1p(jnp.exp(-jnp.abs(x)))


def _top2(logits, n_valid):
    lane = lax.broadcasted_iota(jnp.int32, logits.shape, 1).astype(F32)
    lg = jnp.where(lane < n_valid, logits, NEG_INF)
    m1 = jnp.max(lg, axis=-1, keepdims=True)
    i1 = jnp.min(jnp.where(lg == m1, lane, float(LANES)), axis=-1, keepdims=True)
    lg2 = jnp.where(lane == i1, NEG_INF, lg)
    m2 = jnp.max(lg2, axis=-1, keepdims=True)
    i2 = jnp.min(jnp.where(lg2 == m2, lane, float(LANES)), axis=-1, keepdims=True)
    e2 = jnp.exp(m2 - m1)
    g1 = 1.0 / (1.0 + e2)
    g2 = e2 / (1.0 + e2)
    return jnp.where(lane == 0, i1, jnp.where(lane == 1, i2, jnp.where(lane == 2, g1, jnp.where(lane == 3, g2, 0.0))))


def _norm_head_body(*refs, n_add, out_sum, mode, n_valid):
    x = refs[0][...]
    for r in refs[1:1 + n_add]:
        x = x + r[...]
    g_ref, whi_ref, wlo_ref, b_ref = refs[1 + n_add:5 + n_add]
    outs = refs[5 + n_add:]
    if out_sum:
        outs[0][...] = x
        outs = outs[1:]
    xn_ref, head_ref = outs
    y = _rms(x, g_ref[...])
    xn_ref[...] = y.astype(xn_ref.dtype)
    y_hi, y_lo = _split_bf16(y)
    acc = _dot3(y_hi, y_lo, whi_ref[...], wlo_ref[...]) + b_ref[...]
    if mode == "logf":
        head_ref[...] = _log_sigmoid(acc)
    else:
        head_ref[...] = _top2(acc, n_valid)


def rmsnorm_head(x, adds, g, w_head, b_head, xn_dtype, mode, out_sum):
    t, d = x.shape
    n = w_head.shape[1]
    tm = _pick_tile(t, 256, 8)
    w_pad = jnp.pad(w_head.astype(F32), ((0, 0), (0, LANES - n)))
    w_hi, w_lo = _split_bf16(w_pad)
    b_pad = jnp.pad(b_head.astype(F32), (0, LANES - n)).reshape(1, LANES)
    row = pl.BlockSpec((tm, d), lambda i: (i, 0))
    full = lambda shape: pl.BlockSpec(shape, lambda i: (0, 0))
    shapes = ([jax.ShapeDtypeStruct((t, d), F32)] if out_sum else []) + [
        jax.ShapeDtypeStruct((t, d), xn_dtype), jax.ShapeDtypeStruct((t, LANES), F32)]
    out_specs = ([row] if out_sum else []) + [row, pl.BlockSpec((tm, LANES), lambda i: (i, 0))]
    return pl.pallas_call(
        functools.partial(_norm_head_body, n_add=len(adds), out_sum=out_sum, mode=mode, n_valid=n),
        grid=(t // tm,),
        in_specs=[row] * (1 + len(adds)) + [full((1, d)), full((d, LANES)), full((d, LANES)), full((1, LANES))],
        out_specs=out_specs,
        out_shape=shapes,
        compiler_params=_cparams(1, 48),
        name="rmsnorm_" + mode,
    )(x, *adds, g.reshape(1, d), w_hi, w_lo, b_pad)


def _mm_body(*refs, n_w, has_res, glu, scale, out_kinds):
    x_ref = refs[0]
    w_refs = refs[1:1 + n_w]
    pos = 1 + n_w
    res_ref = refs[pos] if has_res else None
    pos += int(has_res)
    out_refs = refs[pos:pos + len(out_kinds)]
    wbf_refs = refs[pos + len(out_kinds):]

    @pl.when(pl.program_id(1) == 0)
    def _():
        for w_ref, wbf in zip(w_refs, wbf_refs):
            wbf[...] = w_ref[...].astype(BF16)

    x = x_ref[...]
    acc = _dot(x, wbf_refs[0][...])
    if glu:
        acc = acc * jax.nn.sigmoid(_dot(x, wbf_refs[1][...]))
    if has_res:
        acc = acc + res_ref[...]
    for o, kind in zip(out_refs, out_kinds):
        o[...] = acc if kind == "f32" else (acc * scale).astype(BF16)


def matmul(x, ws, col_off, n, res=None, glu=False, scale=1.0, out_kinds=("f32",), tn=256):
    t, k = x.shape
    tm = _pick_tile(t, 1056, 16)
    assert n % tn == 0 and col_off % tn == 0
    off = col_off // tn
    in_specs = [pl.BlockSpec((tm, k), lambda j, i: (i, 0))]
    in_specs += [pl.BlockSpec((k, tn), lambda j, i: (0, j + off))] * len(ws)
    args = [x, *ws]
    if res is not None:
        in_specs.append(pl.BlockSpec((tm, tn), lambda j, i: (i, j)))
        args.append(res)
    out_tile = pl.BlockSpec((tm, tn), lambda j, i: (i, j))
    shapes = [jax.ShapeDtypeStruct((t, n), F32 if kind == "f32" else BF16) for kind in out_kinds]
    outs = pl.pallas_call(
        functools.partial(_mm_body, n_w=len(ws), has_res=res is not None, glu=glu, scale=scale,
                          out_kinds=out_kinds),
        grid=(n // tn, t // tm),
        in_specs=in_specs,
        out_specs=[out_tile] * len(shapes),
        out_shape=shapes,
        scratch_shapes=[pltpu.VMEM((k, tn), BF16) for _ in ws],
        compiler_params=_cparams(2, 56),
        name="glu" if glu else "matmul",
    )(*args)
    return outs


FFN_CHUNK = 256


def _ffn_body(te_ref, tv_ref, x_ref, wg_ref, wu_ref, wd_ref, *rest, gated, n_chunks):
    del te_ref
    if gated:
        gate_ref, o_ref = rest
    else:
        (o_ref,) = rest
    i = pl.program_id(0)
    j = pl.program_id(1)
    valid = tv_ref[i]

    @pl.when(j == 0)
    def _():
        o_ref[...] = jnp.zeros_like(o_ref)

    @pl.when(valid > 0)
    def _():
        wg = wg_ref[...].astype(BF16)
        wu = wu_ref[...].astype(BF16)
        wd = wd_ref[...].astype(BF16)
        for c in range(n_chunks):
            rows = pl.ds(c * FFN_CHUNK, FFN_CHUNK)

            @pl.when(c * FFN_CHUNK < valid)
            def _():
                xs = x_ref[rows, :]
                g = _dot(xs, wg)
                u = _dot(xs, wu)
                h = (g * jax.nn.sigmoid(g) * u).astype(BF16)
                o_ref[rows, :] += _dot(h, wd)

    if gated:
        @pl.when(j == pl.num_programs(1) - 1)
        def _():
            o_ref[...] = o_ref[...] * gate_ref[...]


def swiglu(x, w_gate, w_up, w_down, tile_expert, tile_valid, tm, row_gate=None, tf=256):
    rows, d = x.shape
    n_e, _, f = w_gate.shape
    n_tiles = rows // tm
    n_j = f // tf
    assert rows % tm == 0 and f % tf == 0 and tm % FFN_CHUNK == 0

    def jeff(i, j, tv):
        return jnp.where(tv[i] > 0, j, n_j - 1)

    once = pl.Buffered(1)
    in_specs = [
        pl.BlockSpec((tm, d), lambda i, j, te, tv: (i, 0), pipeline_mode=once),
        pl.BlockSpec((None, d, tf), lambda i, j, te, tv: (te[i], 0, jeff(i, j, tv))),
        pl.BlockSpec((None, d, tf), lambda i, j, te, tv: (te[i], 0, jeff(i, j, tv))),
        pl.BlockSpec((None, tf, d), lambda i, j, te, tv: (te[i], jeff(i, j, tv), 0)),
    ]
    args = [x, w_gate, w_up, w_down]
    if row_gate is not None:
        in_specs.append(pl.BlockSpec((tm, 1), lambda i, j, te, tv: (i, 0)))
        args.append(row_gate)
    return pl.pallas_call(
        functools.partial(_ffn_body, gated=row_gate is not None, n_chunks=tm // FFN_CHUNK),
        grid_spec=pltpu.PrefetchScalarGridSpec(
            num_scalar_prefetch=2,
            grid=(n_tiles, n_j),
            in_specs=in_specs,
            out_specs=pl.BlockSpec((tm, d), lambda i, j, te, tv: (i, 0), pipeline_mode=once),
        ),
        out_shape=jax.ShapeDtypeStruct((rows, d), F32),
        compiler_params=_cparams(2, 62),
        name="swiglu",
    )(tile_expert, tile_valid, *args)


def _s5_tables(a_re, a_im, log_dt, b_re, b_im, c_re, c_im, n_cb):
    hp = lax.Precision.HIGHEST
    g, p = a_re.shape
    c = S5_GROUP_DIM
    gl = LANES // c
    L = S5_BLOCK
    dt = jnp.exp(log_dt)[:, None]
    mag = jnp.exp(a_re * dt)
    lb_re, lb_im = mag * jnp.cos(a_im * dt), mag * jnp.sin(a_im * dt)
    den = a_re * a_re + a_im * a_im
    q_re = ((lb_re - 1.0) * a_re + lb_im * a_im) / den
    q_im = (lb_im * a_re - (lb_re - 1.0) * a_im) / den
    bb_re = q_re[..., None] * b_re - q_im[..., None] * b_im
    bb_im = q_re[..., None] * b_im + q_im[..., None] * b_re
    pw_re, pw_im = [jnp.ones_like(lb_re)], [jnp.zeros_like(lb_re)]
    for _ in range(L):
        r, i = pw_re[-1], pw_im[-1]
        pw_re.append(r * lb_re - i * lb_im)
        pw_im.append(r * lb_im + i * lb_re)
    pw_re, pw_im = jnp.stack(pw_re), jnp.stack(pw_im)
    w_re = pw_re[:L, :, :, None] * bb_re[None] - pw_im[:L, :, :, None] * bb_im[None]
    w_im = pw_re[:L, :, :, None] * bb_im[None] + pw_im[:L, :, :, None] * bb_re[None]
    eye = jnp.eye(gl, dtype=F32)

    taps = (jnp.einsum("gcp,dgpk->dgkc", c_re, w_re, precision=hp)
            - jnp.einsum("gcp,dgpk->dgkc", c_im, w_im, precision=hp))
    s_idx = jnp.arange(L)[:, None]
    i_idx = jnp.arange(L)[None, :]
    kt = jnp.where((i_idx >= s_idx)[:, :, None, None, None], taps[jnp.clip(i_idx - s_idx, 0, L - 1)], 0.0)
    kt = kt.reshape(L, L, n_cb, gl, c, c)
    kt = jnp.einsum("sijgkc,gh->jsgkihc", kt, eye).reshape(n_cb, L * LANES, L * LANES).astype(BF16)

    wend = jnp.stack([w_re[::-1], w_im[::-1]], axis=0)
    wend = wend.reshape(2, L, n_cb, gl, p, c)
    wend = jnp.einsum("rsjgpk,gh->jsgkrhp", wend, eye).reshape(n_cb, L * LANES, 2 * gl * p)
    wend_hi, wend_lo = _split_bf16(wend)

    pr, pi = pw_re[1:], pw_im[1:]
    cc_a = c_re[None] * pr[:, :, None, :] - c_im[None] * pi[:, :, None, :]
    cc_b = -(c_re[None] * pi[:, :, None, :] + c_im[None] * pr[:, :, None, :])
    cc = jnp.stack([cc_a, cc_b], axis=0).reshape(2, L, n_cb, gl, c, p)
    cc = jnp.einsum("rijgcp,gh->jrgpihc", cc, eye).reshape(n_cb, 2 * gl * p, L * LANES).astype(BF16)

    lam_l = (pw_re[L].reshape(1, g * p), pw_im[L].reshape(1, g * p))
    return kt, wend_hi, wend_lo, cc, lam_l


def _gather_blocks(u_ref, dst_ref, nb):
    for s in range(S5_BLOCK):
        dst_ref[:, s * LANES:(s + 1) * LANES] = u_ref[pl.ds(s, nb, stride=S5_BLOCK), :].astype(dst_ref.dtype)


def _s5_state_body(u_ref, whi_ref, wlo_ref, lr_ref, li_ref, h0_ref, cin_ref, fin_ref, ublk, e_scr,
                   *, nb_prompt, nb):
    half = cin_ref.shape[1] // 2
    _gather_blocks(u_ref, ublk, nb)
    u_hi, u_lo = _split_bf16(ublk[...])
    e_scr[...] = _dot3(u_hi, u_lo, whi_ref[...], wlo_ref[...])
    lr, li = lr_ref[...], li_ref[...]

    def step(m, carry):
        cr, ci = carry
        cin_ref[pl.ds(m, 1), :half] = cr
        cin_ref[pl.ds(m, 1), half:] = ci
        er = e_scr[pl.ds(m, 1), :half]
        ei = e_scr[pl.ds(m, 1), half:]
        return er + lr * cr - li * ci, ei + lr * ci + li * cr

    zero = jnp.zeros((1, half), F32)
    cr, ci = lax.fori_loop(0, nb_prompt, step, (zero, zero))
    fin_ref[0:1, :half] = cr
    fin_ref[0:1, half:] = ci
    h0 = h0_ref[...]
    cin_ref[nb_prompt:nb, :] = h0
    hr, hi = h0[:, :half], h0[:, half:]
    es = e_scr[nb_prompt:nb, :]
    fin_ref[1:1 + nb - nb_prompt, :half] = es[:, :half] + lr * hr - li * hi
    fin_ref[1:1 + nb - nb_prompt, half:] = es[:, half:] + lr * hi + li * hr


def _gelu_tanh(x):
    return x * (0.5 * (1.0 + jnp.tanh(math.sqrt(2.0 / math.pi) * (x + 0.044715 * (x * x * x)))))


def _s5_out_body(u_ref, cin_ref, kt_ref, cc_ref, d_ref, z_ref, ublk, yscr, *, nb):
    _gather_blocks(u_ref, ublk, nb)
    yw = _dot(ublk[...], kt_ref[...]) + _dot(cin_ref[...].astype(BF16), cc_ref[...])
    for i in range(S5_BLOCK):
        yscr[pl.ds(i, nb, stride=S5_BLOCK), :] = yw[:, i * LANES:(i + 1) * LANES]
    y = yscr[...] + d_ref[...] * u_ref[...]
    z_ref[...] = _gelu_tanh(y).astype(z_ref.dtype)


def s5_mixer(u, h0_re, h0_im, n_prompt, prm):
    t, d = u.shape
    g, p = prm["s5_a_re"].shape
    n_cb = d // LANES
    gl = LANES // S5_GROUP_DIM
    nb, nb_prompt = t // S5_BLOCK, n_prompt // S5_BLOCK
    nbs = nb - nb_prompt
    sw = 2 * gl * p
    kt, wend_hi, wend_lo, cc, (lam_re, lam_im) = _s5_tables(
        prm["s5_a_re"], prm["s5_a_im"], prm["s5_log_dt"], prm["s5_b_re"], prm["s5_b_im"],
        prm["s5_c_re"], prm["s5_c_im"], n_cb)
    h0 = jnp.concatenate([h0_re.reshape(nbs, n_cb, gl * p), h0_im.reshape(nbs, n_cb, gl * p)],
                         axis=-1).reshape(nbs, n_cb * sw)
    kw = S5_BLOCK * LANES
    ucol = pl.BlockSpec((t, LANES), lambda j: (0, j))
    cin, fin = pl.pallas_call(
        functools.partial(_s5_state_body, nb_prompt=nb_prompt, nb=nb),
        grid=(n_cb,),
        in_specs=[ucol,
                  pl.BlockSpec((None, kw, sw), lambda j: (j, 0, 0)),
                  pl.BlockSpec((None, kw, sw), lambda j: (j, 0, 0)),
                  pl.BlockSpec((1, sw // 2), lambda j: (0, j)),
                  pl.BlockSpec((1, sw // 2), lambda j: (0, j)),
                  pl.BlockSpec((nbs, sw), lambda j: (0, j))],
        out_specs=[pl.BlockSpec((nb, sw), lambda j: (0, j)),
                   pl.BlockSpec((1 + nbs, sw), lambda j: (0, j))],
        out_shape=[jax.ShapeDtypeStruct((nb, n_cb * sw), F32),
                   jax.ShapeDtypeStruct((1 + nbs, n_cb * sw), F32)],
        scratch_shapes=[pltpu.VMEM((nb, kw), F32), pltpu.VMEM((nb, sw), F32)],
        compiler_params=_cparams(1, 56),
        name="s5_state",
    )(u, wend_hi, wend_lo, lam_re, lam_im, h0)
    z = pl.pallas_call(
        functools.partial(_s5_out_body, nb=nb),
        grid=(n_cb,),
        in_specs=[ucol,
                  pl.BlockSpec((nb, sw), lambda j: (0, j)),
                  pl.BlockSpec((None, kw, kw), lambda j: (j, 0, 0)),
                  pl.BlockSpec((None, sw, kw), lambda j: (j, 0, 0)),
                  pl.BlockSpec((1, LANES), lambda j: (0, j))],
        out_specs=pl.BlockSpec((t, LANES), lambda j: (0, j)),
        out_shape=jax.ShapeDtypeStruct((t, d), BF16),
        scratch_shapes=[pltpu.VMEM((nb, kw), BF16), pltpu.VMEM((t, LANES), F32)],
        compiler_params=_cparams(1, 56),
        name="s5_out",
    )(u, cin, kt, cc, prm["s5_d"].reshape(1, d))
    fin = fin.reshape(1 + nbs, n_cb, 2, gl * p)
    fin_re = fin[:, :, 0].reshape(1 + nbs, g, p)
    fin_im = fin[:, :, 1].reshape(1 + nbs, g, p)
    return z, fin_re[:1], fin_im[:1], fin_re[1:], fin_im[1:]


def _cumsum_body(x_ref, init_ref, o_ref, *, n_rows):
    def step(t, carry):
        carry = carry + x_ref[pl.ds(t, 1), :]
        o_ref[pl.ds(t, 1), :] = carry
        return carry

    lax.fori_loop(0, n_rows, step, init_ref[...], unroll=8)


def cumsum_rows(x, init):
    r, c = x.shape
    return pl.pallas_call(
        functools.partial(_cumsum_body, n_rows=r),
        out_shape=jax.ShapeDtypeStruct((r, c), F32),
        compiler_params=pltpu.CompilerParams(vmem_limit_bytes=40 << 20),
        name="cumsum_rows",
    )(x, init)


def _nt_dot(a, b):
    return lax.dot_general(a, b, (((1,), (1,)), ((), ())), preferred_element_type=F32)


def _fox_prompt_body(q_ref, k_ref, v_ref, cq_ref, ck_ref, o_ref, m_scr, l_scr, acc_scr, *, blk, hb):
    qi = pl.program_id(1)
    ki = pl.program_id(2)

    @pl.when(ki == 0)
    def _():
        m_scr[...] = jnp.full_like(m_scr, NEG_INF)
        l_scr[...] = jnp.zeros_like(l_scr)
        acc_scr[...] = jnp.zeros_like(acc_scr)

    def process(masked):
        for h in range(hb):
            cols = slice(h * HEAD_DIM, (h + 1) * HEAD_DIM)
            s = _nt_dot(q_ref[:, cols], k_ref[:, cols]) + cq_ref[h] - ck_ref[h]
            if masked:
                row = lax.broadcasted_iota(jnp.int32, s.shape, 0)
                col = lax.broadcasted_iota(jnp.int32, s.shape, 1)
                s = jnp.where(col <= row, s, NEG_INF)
            m_prev = m_scr[h]
            m_new = jnp.maximum(m_prev, jnp.max(s, axis=-1, keepdims=True))
            alpha = jnp.exp(m_prev - m_new)
            p = jnp.exp(s - m_new)
            l_scr[h] = alpha * l_scr[h] + jnp.sum(p, axis=-1, keepdims=True)
            acc_scr[h] = alpha * acc_scr[h] + _dot(p.astype(BF16), v_ref[:, cols])
            m_scr[h] = m_new

    @pl.when(ki < qi)
    def _():
        process(False)

    @pl.when(ki == qi)
    def _():
        process(True)

    @pl.when(ki == pl.num_programs(2) - 1)
    def _():
        for h in range(hb):
            o_ref[:, h * HEAD_DIM:(h + 1) * HEAD_DIM] = (acc_scr[h] / l_scr[h]).astype(o_ref.dtype)


def fox_prompt_attention(q, k, v, c, n_prompt, blk=512, hb=2):
    d = q.shape[1]
    n_h = d // HEAD_DIM
    blk = _pick_tile(n_prompt, blk, 16)
    hb = min(hb, n_h)
    nq = n_prompt // blk
    cq = c.T.reshape(n_h, n_prompt, 1)
    ck = c.T.reshape(n_h, 1, n_prompt)
    kv_spec = pl.BlockSpec((blk, hb * HEAD_DIM), lambda h, qi, ki: (jnp.minimum(ki, qi), h))
    return pl.pallas_call(
        functools.partial(_fox_prompt_body, blk=blk, hb=hb),
        grid=(n_h // hb, nq, nq),
        in_specs=[pl.BlockSpec((blk, hb * HEAD_DIM), lambda h, qi, ki: (qi, h)),
                  kv_spec, kv_spec,
                  pl.BlockSpec((hb, blk, 1), lambda h, qi, ki: (h, qi, 0)),
                  pl.BlockSpec((hb, 1, blk), lambda h, qi, ki: (h, 0, jnp.minimum(ki, qi)))],
        out_specs=pl.BlockSpec((blk, hb * HEAD_DIM), lambda h, qi, ki: (qi, h)),
        out_shape=jax.ShapeDtypeStruct((n_prompt, d), BF16),
        scratch_shapes=[pltpu.VMEM((hb, blk, 1), F32), pltpu.VMEM((hb, blk, 1), F32),
                        pltpu.VMEM((hb, blk, HEAD_DIM), F32)],
        compiler_params=_cparams(3, 48),
        name="fox_prompt",
    )(q, k, v, cq, ck)


def _fox_sample_body(q_ref, k_ref, v_ref, kc_ref, vc_ref, cq_ref, ckn_ref, ckp_ref, o_ref, *, hb):
    for h in range(hb):
        cols = slice(h * HEAD_DIM, (h + 1) * HEAD_DIM)
        q = q_ref[:, cols]
        cq = cq_ref[h]
        s_past = _nt_dot(q, kc_ref[:, cols].astype(BF16)) + cq - ckp_ref[h]
        s_new = _nt_dot(q, k_ref[:, cols]) + cq - ckn_ref[h]
        row = lax.broadcasted_iota(jnp.int32, s_new.shape, 0)
        col = lax.broadcasted_iota(jnp.int32, s_new.shape, 1)
        s_new = jnp.where(col <= row, s_new, NEG_INF)
        m = jnp.maximum(jnp.max(s_past, axis=-1, keepdims=True), jnp.max(s_new, axis=-1, keepdims=True))
        p_past = jnp.exp(s_past - m)
        p_new = jnp.exp(s_new - m)
        l = jnp.sum(p_past, axis=-1, keepdims=True) + jnp.sum(p_new, axis=-1, keepdims=True)
        o = _dot(p_past.astype(BF16), vc_ref[:, cols].astype(BF16)) + _dot(p_new.astype(BF16), v_ref[:, cols])
        o_ref[:, cols] = (o / l).astype(o_ref.dtype)


def fox_sample_attention(q, k, v, k_cache, v_cache, c_past, c_new, n_prompt, hb=2):
    bsz, past, n_h, _ = k_cache.shape
    d = n_h * HEAD_DIM
    seq = c_new.shape[-1]
    hb = min(hb, n_h)
    row0 = n_prompt // seq
    kc = k_cache.reshape(bsz, past, d)
    vc = v_cache.reshape(bsz, past, d)
    new_spec = pl.BlockSpec((seq, hb * HEAD_DIM), lambda b, h: (row0 + b, h))
    cache_spec = pl.BlockSpec((None, past, hb * HEAD_DIM), lambda b, h: (b, 0, h))
    return pl.pallas_call(
        functools.partial(_fox_sample_body, hb=hb),
        grid=(bsz, n_h // hb),
        in_specs=[new_spec, new_spec, new_spec, cache_spec, cache_spec,
                  pl.BlockSpec((None, hb, seq, 1), lambda b, h: (b, h, 0, 0)),
                  pl.BlockSpec((None, hb, 1, seq), lambda b, h: (b, h, 0, 0)),
                  pl.BlockSpec((None, hb, 1, past), lambda b, h: (b, h, 0, 0))],
        out_specs=pl.BlockSpec((seq, hb * HEAD_DIM), lambda b, h: (b, h)),
        out_shape=jax.ShapeDtypeStruct((bsz * seq, d), BF16),
        compiler_params=_cparams(2, 56),
        name="fox_sample",
    )(q, k, v, kc, vc, c_new[..., None], c_new[:, :, None, :], c_past)


GATHER_ROWS = 256


def _gather_body(idx_ref, x_hbm, o_ref, buf, sem):
    base = pl.program_id(0) * GATHER_ROWS

    def row_copy(r):
        return pltpu.make_async_copy(x_hbm.at[pl.ds(idx_ref[base + r], 1), :], buf.at[pl.ds(r, 1), :], sem)

    def start(r, c):
        row_copy(r).start()
        return c

    def wait(r, c):
        row_copy(r).wait()
        return c

    lax.fori_loop(0, GATHER_ROWS, start, 0)
    lax.fori_loop(0, GATHER_ROWS, wait, 0)
    o_ref[...] = buf[...].astype(o_ref.dtype)


def gather_rows(x, idx, out_dtype):
    n_out = idx.shape[0]
    d = x.shape[1]
    return pl.pallas_call(
        _gather_body,
        grid_spec=pltpu.PrefetchScalarGridSpec(
            num_scalar_prefetch=1,
            grid=(n_out // GATHER_ROWS,),
            in_specs=[pl.BlockSpec(memory_space=pl.ANY)],
            out_specs=pl.BlockSpec((GATHER_ROWS, d), lambda i, idx: (i, 0)),
            scratch_shapes=[pltpu.VMEM((GATHER_ROWS, d), F32), pltpu.SemaphoreType.DMA(())],
        ),
        out_shape=jax.ShapeDtypeStruct((n_out, d), out_dtype),
        compiler_params=_cparams(1, 32),
        name="gather_rows",
    )(idx, x)


COMBINE_ROWS = 128


def _combine_body(pos_ref, x_ref, g_ref, y_hbm, o_ref, buf, sem):
    base = pl.program_id(0) * COMBINE_ROWS

    def row_copy(r, slot):
        src = y_hbm.at[pl.ds(pos_ref[2 * (base + r) + slot], 1), :]
        return pltpu.make_async_copy(src, buf.at[slot, pl.ds(r, 1), :], sem)

    def start(r, c):
        row_copy(r, 0).start()
        row_copy(r, 1).start()
        return c

    def wait(r, c):
        row_copy(r, 0).wait()
        row_copy(r, 1).wait()
        return c

    lax.fori_loop(0, COMBINE_ROWS, start, 0)
    lax.fori_loop(0, COMBINE_ROWS, wait, 0)
    o_ref[...] = _rms(x_ref[...] + (buf[0] + buf[1]), g_ref[...])


def combine_norm(x, y_sorted, pos, g):
    t, d = x.shape
    assert t % COMBINE_ROWS == 0
    return pl.pallas_call(
        _combine_body,
        grid_spec=pltpu.PrefetchScalarGridSpec(
            num_scalar_prefetch=1,
            grid=(t // COMBINE_ROWS,),
            in_specs=[pl.BlockSpec((COMBINE_ROWS, d), lambda i, pos: (i, 0)),
                      pl.BlockSpec((1, d), lambda i, pos: (0, 0)),
                      pl.BlockSpec(memory_space=pl.ANY)],
            out_specs=pl.BlockSpec((COMBINE_ROWS, d), lambda i, pos: (i, 0)),
            scratch_shapes=[pltpu.VMEM((2, COMBINE_ROWS, d), F32), pltpu.SemaphoreType.DMA(())],
        ),
        out_shape=jax.ShapeDtypeStruct((t, d), F32),
        compiler_params=_cparams(1, 32),
        name="combine_norm",
    )(pos, x, g.reshape(1, d), y_sorted)


def _route(top_idx, top_gate, n_experts, tm):
    t = top_idx.shape[0]
    n_items = t * top_idx.shape[1]
    n_tiles = (n_items + n_experts * (tm - 1)) // tm
    flat_e = top_idx.reshape(-1)
    order = jnp.argsort(flat_e, stable=True)
    counts = jnp.zeros((n_experts,), jnp.int32).at[flat_e].add(1)
    tiles_e = (counts + tm - 1) // tm
    tile_start = jnp.cumsum(tiles_e) - tiles_e
    item_start = jnp.cumsum(counts) - counts
    e_sorted = flat_e[order]
    dest_sorted = tile_start[e_sorted] * tm + (jnp.arange(n_items, dtype=jnp.int32) - item_start[e_sorted])
    pos = jnp.zeros((n_items,), jnp.int32).at[order].set(dest_sorted)
    rows = n_tiles * tm
    row_token = jnp.zeros((rows,), jnp.int32).at[pos].set(jnp.arange(n_items, dtype=jnp.int32) // top_idx.shape[1])
    row_gate = jnp.zeros((rows,), F32).at[pos].set(top_gate.reshape(-1)).reshape(rows, 1)
    tile_ids = jnp.arange(n_tiles, dtype=jnp.int32)
    n_used = jnp.sum(tiles_e)
    tile_expert = jnp.sum(tile_ids[:, None] >= (tile_start + tiles_e)[None, :], axis=1).astype(jnp.int32)
    last_e = jnp.max(jnp.where(counts > 0, jnp.arange(n_experts), 0)).astype(jnp.int32)
    tile_expert = jnp.where(tile_ids < n_used, jnp.minimum(tile_expert, n_experts - 1), last_e)
    within = tile_ids - tile_start[tile_expert]
    tile_valid = jnp.where(tile_ids < n_used, jnp.clip(counts[tile_expert] - within * tm, 0, tm), 0).astype(jnp.int32)
    return row_token, row_gate, pos, tile_expert, tile_valid


def kernel(x_prompt, x_sample, state_s5_re, state_s5_im, cache_fox_k, cache_fox_v, cache_fox_logf, norm_mix0, s5_a_re, s5_a_im, s5_log_dt, s5_b_re, s5_b_im, s5_c_re, s5_c_im, s5_d, s5_w_glu_a, s5_w_glu_b, norm_ffn0, ffn_w_gate, ffn_w_up, ffn_w_down, norm_mix1, fox_w_in, fox_b_f, fox_w_o, norm_ffn1, moe_router, moe_w_gate, moe_w_up, moe_w_down, norm_final):
    bp, n_prompt, d = x_prompt.shape
    bs, seq, _ = x_sample.shape
    assert bp == 1 and seq == S5_BLOCK and n_prompt % S5_BLOCK == 0
    n_h = d // HEAD_DIM
    n_e = moe_router.shape[1]
    past = cache_fox_k.shape[1]
    x = jnp.concatenate([x_prompt.reshape(n_prompt, d), x_sample.reshape(bs * seq, d)], axis=0)
    t = x.shape[0]
    s5_prm = dict(s5_a_re=s5_a_re, s5_a_im=s5_a_im, s5_log_dt=s5_log_dt, s5_b_re=s5_b_re, s5_b_im=s5_b_im,
                  s5_c_re=s5_c_re, s5_c_im=s5_c_im, s5_d=s5_d)

    (u0,) = rmsnorm(x, [], norm_mix0, [F32])
    z, s5_re_p, s5_im_p, s5_re_s, s5_im_s = s5_mixer(u0, state_s5_re, state_s5_im, n_prompt, s5_prm)
    (xa,) = matmul(z, [s5_w_glu_a, s5_w_glu_b], 0, d, res=x, glu=True)
    (un0,) = rmsnorm(xa, [], norm_ffn0, [BF16])
    tm_dense = _pick_tile(t, 768, FFN_CHUNK)
    n_dense = t // tm_dense
    f0 = swiglu(un0, ffn_w_gate[None], ffn_w_up[None], ffn_w_down[None],
                jnp.zeros((n_dense,), jnp.int32), jnp.full((n_dense,), tm_dense, jnp.int32), tm_dense)

    x1, un1, logf_pad = rmsnorm_head(xa, [f0], norm_mix1, fox_w_in[:, 3 * d:], fox_b_f, BF16, "logf", True)
    logf = logf_pad[:, :n_h]
    scale = HEAD_DIM ** -0.5
    (q,) = matmul(un1, [fox_w_in], 0, d, scale=scale, out_kinds=("bf16",))
    k_f, k_b = matmul(un1, [fox_w_in], d, d, out_kinds=("f32", "bf16"))
    v_f, v_b = matmul(un1, [fox_w_in], 2 * d, d, out_kinds=("f32", "bf16"))

    c_prompt = cumsum_rows(logf[:n_prompt], jnp.zeros((1, n_h), F32))
    o_p = fox_prompt_attention(q, k_b, v_b, c_prompt, n_prompt)
    lc = cache_fox_logf.astype(F32).transpose(1, 0, 2).reshape(past, bs * n_h)
    c_past = cumsum_rows(lc, jnp.zeros((1, bs * n_h), F32))
    ls = logf[n_prompt:].reshape(bs, seq, n_h).transpose(1, 0, 2).reshape(seq, bs * n_h)
    c_new = cumsum_rows(ls, c_past[past - 1:past])
    c_past_b = c_past.reshape(past, bs, n_h).transpose(1, 2, 0).reshape(bs, n_h, 1, past)
    c_new_b = c_new.reshape(seq, bs, n_h).transpose(1, 2, 0)
    o_s = fox_sample_attention(q, k_b, v_b, cache_fox_k, cache_fox_v, c_past_b, c_new_b, n_prompt)
    o = jnp.concatenate([o_p, o_s], axis=0)
    (xb,) = matmul(o, [fox_w_o], 0, d, res=x1)

    un2, route = rmsnorm_head(xb, [], norm_ffn1, moe_router, jnp.zeros((n_e,), F32), F32, "router", False)
    top_idx = route[:, :2].astype(jnp.int32)
    top_gate = route[:, 2:4]
    tm_moe = 1024 if t * 2 >= 4096 else FFN_CHUNK
    row_token, row_gate, pos, tile_expert, tile_valid = _route(top_idx, top_gate, n_e, tm_moe)
    x_sorted = gather_rows(un2, row_token, BF16)
    y_sorted = swiglu(x_sorted, moe_w_gate, moe_w_up, moe_w_down, tile_expert, tile_valid, tm_moe, row_gate=row_gate)
    y = combine_norm(xb, y_sorted, pos, norm_final)

    k4 = lambda a, b: a.reshape(b, -1, n_h, HEAD_DIM)
    return (y[:n_prompt].reshape(bp, n_prompt, d), y[n_prompt:].reshape(bs, seq, d),
            s5_re_p, s5_im_p, s5_re_s, s5_im_s,
            k4(k_f[:n_prompt], bp), k4(v_f[:n_prompt], bp), logf[:n_prompt].reshape(bp, n_prompt, n_h),
            k4(k_f[n_prompt:], bs), k4(v_f[n_prompt:], bs), logf[n_prompt:].reshape(bs, seq, n_h))
```

```python
import functools
import math

import jax
import jax.numpy as jnp
from jax import lax
from jax.experimental import pallas as pl
from jax.experimental.pallas import tpu as pltpu

F32 = jnp.float32
BF16 = jnp.bfloat16

RMS_EPS = 1e-5
NEG_INF = -1e30
S5_GROUP_DIM = 16
S5_STATE = 64
S5_BLOCK = 16
HEAD_DIM = 128
LANES = 128
V7X_VMEM_BYTES = 64 << 20


def _cparams(n_grid, vmem_mb):
    return pltpu.CompilerParams(dimension_semantics=("arbitrary",) * n_grid,
                                vmem_limit_bytes=min(vmem_mb << 20, V7X_VMEM_BYTES - (2 << 20)))


def _pick_tile(n, cap, mult):
    for t in range(min(cap, n), 0, -1):
        if n % t == 0 and t % mult == 0:
            return t
    raise ValueError(f"no tile for {n} (cap {cap}, multiple of {mult})")


def _split_bf16(x):
    hi = x.astype(BF16)
    lo = (x - hi.astype(F32)).astype(BF16)
    return hi, lo


def _dot(a, b):
    return jnp.dot(a, b, preferred_element_type=F32)


def _dot3(a_hi, a_lo, b_hi, b_lo):
    return _dot(a_hi, b_hi) + _dot(a_lo, b_hi) + _dot(a_hi, b_lo)


def _rms(x, g):
    y = x * lax.rsqrt(jnp.mean(x * x, axis=-1, keepdims=True) + RMS_EPS)
    return y * g


def _norm_body(*refs, n_add, out_sum, n_out):
    x = refs[0][...]
    for r in refs[1:1 + n_add]:
        x = x + r[...]
    g_ref = refs[1 + n_add]
    outs = refs[2 + n_add:]
    if out_sum:
        outs[0][...] = x
        outs = outs[1:]
    y = _rms(x, g_ref[...])
    for o in outs:
        o[...] = y.astype(o.dtype)


def rmsnorm(x, adds, g, out_dtypes, out_sum=False):
    t, d = x.shape
    tm = _pick_tile(t, 256, 8)
    row = pl.BlockSpec((tm, d), lambda i: (i, 0))
    shapes = ([jax.ShapeDtypeStruct((t, d), F32)] if out_sum else []) + [
        jax.ShapeDtypeStruct((t, d), dt) for dt in out_dtypes]
    return pl.pallas_call(
        functools.partial(_norm_body, n_add=len(adds), out_sum=out_sum, n_out=len(out_dtypes)),
        grid=(t // tm,),
        in_specs=[row] * (1 + len(adds)) + [pl.BlockSpec((1, d), lambda i: (0, 0))],
        out_specs=[row] * len(shapes),
        out_shape=shapes,
        compiler_params=_cparams(1, 48),
        name="rmsnorm",
    )(x, *adds, g.reshape(1, d))


def _log_sigmoid(x):
    return jnp.minimum(x, 0.0) - jnp.log1p(jnp.exp(-jnp.abs(x)))


def _top2(logits, n_valid):
    lane = lax.broadcasted_iota(jnp.int32, logits.shape, 1).astype(F32)
    lg = jnp.where(lane < n_valid, logits, NEG_INF)
    m1 = jnp.max(lg, axis=-1, keepdims=True)
    i1 = jnp.min(jnp.where(lg == m1, lane, float(LANES)), axis=-1, keepdims=True)
    lg2 = jnp.where(lane == i1, NEG_INF, lg)
    m2 = jnp.max(lg2, axis=-1, keepdims=True)
    i2 = jnp.min(jnp.where(lg2 == m2, lane, float(LANES)), axis=-1, keepdims=True)
    e2 = jnp.exp(m2 - m1)
    g1 = 1.0 / (1.0 + e2)
    g2 = e2 / (1.0 + e2)
    return jnp.where(lane == 0, i1, jnp.where(lane == 1, i2, jnp.where(lane == 2, g1, jnp.where(lane == 3, g2, 0.0))))


def _norm_head_body(*refs, n_add, out_sum, mode, n_valid):
    x = refs[0][...]
    for r in refs[1:1 + n_add]:
        x = x + r[...]
    g_ref, whi_ref, wlo_ref, b_ref = refs[1 + n_add:5 + n_add]
    outs = refs[5 + n_add:]
    if out_sum:
        outs[0][...] = x
        outs = outs[1:]
    xn_ref, head_ref = outs
    y = _rms(x, g_ref[...])
    xn_ref[...] = y.astype(xn_ref.dtype)
    y_hi, y_lo = _split_bf16(y)
    acc = _dot3(y_hi, y_lo, whi_ref[...], wlo_ref[...]) + b_ref[...]
    if mode == "logf":
        head_ref[...] = _log_sigmoid(acc)
    else:
        head_ref[...] = _top2(acc, n_valid)


def rmsnorm_head(x, adds, g, w_head, b_head, xn_dtype, mode, out_sum):
    t, d = x.shape
    n = w_head.shape[1]
    tm = _pick_tile(t, 256, 8)
    w_pad = jnp.pad(w_head.astype(F32), ((0, 0), (0, LANES - n)))
    w_hi, w_lo = _split_bf16(w_pad)
    b_pad = jnp.pad(b_head.astype(F32), (0, LANES - n)).reshape(1, LANES)
    row = pl.BlockSpec((tm, d), lambda i: (i, 0))
    full = lambda shape: pl.BlockSpec(shape, lambda i: (0, 0))
    shapes = ([jax.ShapeDtypeStruct((t, d), F32)] if out_sum else []) + [
        jax.ShapeDtypeStruct((t, d), xn_dtype), jax.ShapeDtypeStruct((t, LANES), F32)]
    out_specs = ([row] if out_sum else []) + [row, pl.BlockSpec((tm, LANES), lambda i: (i, 0))]
    return pl.pallas_call(
        functools.partial(_norm_head_body, n_add=len(adds), out_sum=out_sum, mode=mode, n_valid=n),
        grid=(t // tm,),
        in_specs=[row] * (1 + len(adds)) + [full((1, d)), full((d, LANES)), full((d, LANES)), full((1, LANES))],
        out_specs=out_specs,
        out_shape=shapes,
        compiler_params=_cparams(1, 48),
        name="rmsnorm_" + mode,
    )(x, *adds, g.reshape(1, d), w_hi, w_lo, b_pad)


def _mm_body(*refs, n_w, has_res, glu, scale, out_kinds):
    x_ref = refs[0]
    w_refs = refs[1:1 + n_w]
    pos = 1 + n_w
    res_ref = refs[pos] if has_res else None
    pos += int(has_res)
    out_refs = refs[pos:pos + len(out_kinds)]
    wbf_refs = refs[pos + len(out_kinds):]

    @pl.when(pl.program_id(1) == 0)
    def _():
        for w_ref, wbf in zip(w_refs, wbf_refs):
            wbf[...] = w_ref[...].astype(BF16)

    x = x_ref[...]
    acc = _dot(x, wbf_refs[0][...])
    if glu:
        acc = acc * jax.nn.sigmoid(_dot(x, wbf_refs[1][...]))
    if has_res:
        acc = acc + res_ref[...]
    for o, kind in zip(out_refs, out_kinds):
        o[...] = acc if kind == "f32" else (acc * scale).astype(BF16)


def matmul(x, ws, col_off, n, res=None, glu=False, scale=1.0, out_kinds=("f32",)):
    t, k = x.shape
    tm = _pick_tile(t, 1056, 16)
    tn = 512 // len(ws)
    assert n % tn == 0 and col_off % tn == 0
    off = col_off // tn
    in_specs = [pl.BlockSpec((tm, k), lambda j, i: (i, 0))]
    in_specs += [pl.BlockSpec((k, tn), lambda j, i: (0, j + off))] * len(ws)
    args = [x, *ws]
    if res is not None:
        in_specs.append(pl.BlockSpec((tm, tn), lambda j, i: (i, j)))
        args.append(res)
    out_tile = pl.BlockSpec((tm, tn), lambda j, i: (i, j))
    shapes = [jax.ShapeDtypeStruct((t, n), F32 if kind == "f32" else BF16) for kind in out_kinds]
    outs = pl.pallas_call(
        functools.partial(_mm_body, n_w=len(ws), has_res=res is not None, glu=glu, scale=scale,
                          out_kinds=out_kinds),
        grid=(n // tn, t // tm),
        in_specs=in_specs,
        out_specs=[out_tile] * len(shapes),
        out_shape=shapes,
        scratch_shapes=[pltpu.VMEM((k, tn), BF16) for _ in ws],
        compiler_params=_cparams(2, 56),
        name="glu" if glu else "matmul",
    )(*args)
    return outs


FFN_CHUNK = 256
TOP_K = 2


def _moe_tile(t, n_experts):
    return -(-int(t * TOP_K / n_experts / 2 * 1.09) // 64) * 64


def _ffn_body(te_ref, tv_ref, x_ref, wg_ref, wu_ref, wd_ref, *rest, gated):
    del te_ref
    if gated:
        gate_ref, o_ref = rest
    else:
        (o_ref,) = rest
    i = pl.program_id(0)
    j = pl.program_id(1)
    valid = tv_ref[i]

    @pl.when(j == 0)
    def _():
        o_ref[...] = jnp.zeros_like(o_ref)

    def weights():
        return wg_ref[...].astype(BF16), wu_ref[...].astype(BF16), wd_ref[...].astype(BF16)

    def accumulate(rows, wg, wu, wd):
        xs = x_ref[rows, :]
        g = _dot(xs, wg)
        u = _dot(xs, wu)
        h = (g * jax.nn.sigmoid(g) * u).astype(BF16)
        o_ref[rows, :] += _dot(h, wd)

    tm = x_ref.shape[0]
    for stop in range(FFN_CHUNK, tm + FFN_CHUNK, FFN_CHUNK):
        @pl.when((valid > stop - FFN_CHUNK) & (valid <= stop))
        def _():
            accumulate(pl.ds(0, min(stop, tm)), *weights())

    if gated:
        @pl.when(j == pl.num_programs(1) - 1)
        def _():
            o_ref[...] = o_ref[...] * gate_ref[...]


def swiglu(x, w_gate, w_up, w_down, tile_expert, tile_valid, tm, row_gate=None, tf=256):
    rows, d = x.shape
    n_e, _, f = w_gate.shape
    n_tiles = rows // tm
    n_j = f // tf
    assert rows % tm == 0 and f % tf == 0 and tm % 16 == 0

    def jeff(i, j, tv):
        return jnp.where(tv[i] > 0, j, n_j - 1)

    once = pl.Buffered(1)
    in_specs = [
        pl.BlockSpec((tm, d), lambda i, j, te, tv: (i, 0), pipeline_mode=once),
        pl.BlockSpec((None, d, tf), lambda i, j, te, tv: (te[i], 0, jeff(i, j, tv))),
        pl.BlockSpec((None, d, tf), lambda i, j, te, tv: (te[i], 0, jeff(i, j, tv))),
        pl.BlockSpec((None, tf, d), lambda i, j, te, tv: (te[i], jeff(i, j, tv), 0)),
    ]
    args = [x, w_gate, w_up, w_down]
    if row_gate is not None:
        in_specs.append(pl.BlockSpec((tm, 1), lambda i, j, te, tv: (i, 0)))
        args.append(row_gate)
    return pl.pallas_call(
        functools.partial(_ffn_body, gated=row_gate is not None),
        grid_spec=pltpu.PrefetchScalarGridSpec(
            num_scalar_prefetch=2,
            grid=(n_tiles, n_j),
            in_specs=in_specs,
            out_specs=pl.BlockSpec((tm, d), lambda i, j, te, tv: (i, 0), pipeline_mode=once),
        ),
        out_shape=jax.ShapeDtypeStruct((rows, d), F32),
        compiler_params=_cparams(2, 62),
        name="swiglu",
    )(tile_expert, tile_valid, *args)


def _s5_tables(a_re, a_im, log_dt, b_re, b_im, c_re, c_im, n_cb):
    hp = lax.Precision.HIGHEST
    g, p = a_re.shape
    c = S5_GROUP_DIM
    gl = LANES // c
    L = S5_BLOCK
    dt = jnp.exp(log_dt)[:, None]
    mag = jnp.exp(a_re * dt)
    lb_re, lb_im = mag * jnp.cos(a_im * dt), mag * jnp.sin(a_im * dt)
    den = a_re * a_re + a_im * a_im
    q_re = ((lb_re - 1.0) * a_re + lb_im * a_im) / den
    q_im = (lb_im * a_re - (lb_re - 1.0) * a_im) / den
    bb_re = q_re[..., None] * b_re - q_im[..., None] * b_im
    bb_im = q_re[..., None] * b_im + q_im[..., None] * b_re
    pw_re, pw_im = [jnp.ones_like(lb_re)], [jnp.zeros_like(lb_re)]
    for _ in range(L):
        r, i = pw_re[-1], pw_im[-1]
        pw_re.append(r * lb_re - i * lb_im)
        pw_im.append(r * lb_im + i * lb_re)
    pw_re, pw_im = jnp.stack(pw_re), jnp.stack(pw_im)
    w_re = pw_re[:L, :, :, None] * bb_re[None] - pw_im[:L, :, :, None] * bb_im[None]
    w_im = pw_re[:L, :, :, None] * bb_im[None] + pw_im[:L, :, :, None] * bb_re[None]
    eye = jnp.eye(gl, dtype=F32)

    taps = (jnp.einsum("gcp,dgpk->dgkc", c_re, w_re, precision=hp)
            - jnp.einsum("gcp,dgpk->dgkc", c_im, w_im, precision=hp))
    krev = taps[::-1].reshape(L, n_cb, gl, c, c)
    krev = jnp.einsum("rjgkc,gh->jrgkhc", krev, eye).reshape(n_cb, L * LANES, LANES).astype(BF16)

    wend = jnp.stack([w_re[::-1], w_im[::-1]], axis=0).reshape(2, L, n_cb, gl, p, c)
    wend = jnp.transpose(wend, (2, 0, 1, 3, 5, 4))
    wend = jnp.broadcast_to(wend[..., None, :], (n_cb, 2, L, gl, c, LANES // p, p))
    wend = wend.reshape(n_cb, 2, L * LANES, LANES)

    cmat = jnp.stack([c_re, -c_im], axis=0).reshape(2, n_cb, gl, c, p)
    cmat = jnp.einsum("rjgcp,gh->jrgphc", cmat, eye).reshape(n_cb, 2 * gl * p, LANES).astype(BF16)

    return krev, wend, cmat, pw_re.reshape(L + 1, g * p), pw_im.reshape(L + 1, g * p)


def _gather_blocks(u_ref, dst_ref, nb):
    for s in range(S5_BLOCK):
        dst_ref[:, s * LANES:(s + 1) * LANES] = u_ref[pl.ds(s, nb, stride=S5_BLOCK), :].astype(dst_ref.dtype)


def _s5_state_body(u_ref, w_ref, lr_ref, li_ref, h0_ref, cin_ref, fin_ref, ublk, whi, wlo, e_scr,
                   *, nb_prompt, nb):
    half = cin_ref.shape[1] // 2
    n_pair = half // LANES
    row_g = (lax.broadcasted_iota(jnp.int32, (w_ref.shape[1], LANES), 0) // S5_GROUP_DIM) % (LANES // S5_GROUP_DIM)
    lane_par = lax.broadcasted_iota(jnp.int32, (w_ref.shape[1], LANES), 1) // S5_STATE
    for r in range(2):
        w = w_ref[r]
        for hp in range(n_pair):
            blk = jnp.where(row_g == 2 * hp + lane_par, w, 0.0)
            b_hi, b_lo = _split_bf16(blk)
            cols = slice((r * n_pair + hp) * LANES, (r * n_pair + hp + 1) * LANES)
            whi[:, cols] = b_hi
            wlo[:, cols] = b_lo
    _gather_blocks(u_ref, ublk, nb)
    u_hi, u_lo = _split_bf16(ublk[...])
    e_scr[...] = _dot3(u_hi, u_lo, whi[...], wlo[...])
    lr, li = lr_ref[...], li_ref[...]

    def step(m, carry):
        cr, ci = carry
        cin_ref[pl.ds(m, 1), :half] = cr
        cin_ref[pl.ds(m, 1), half:] = ci
        er = e_scr[pl.ds(m, 1), :half]
        ei = e_scr[pl.ds(m, 1), half:]
        return er + lr * cr - li * ci, ei + lr * ci + li * cr

    zero = jnp.zeros((1, half), F32)
    cr, ci = lax.fori_loop(0, nb_prompt, step, (zero, zero))
    fin_ref[0:1, :half] = cr
    fin_ref[0:1, half:] = ci
    h0 = h0_ref[...]
    cin_ref[nb_prompt:nb, :] = h0
    hr, hi = h0[:, :half], h0[:, half:]
    es = e_scr[nb_prompt:nb, :]
    fin_ref[1:1 + nb - nb_prompt, :half] = es[:, :half] + lr * hr - li * hi
    fin_ref[1:1 + nb - nb_prompt, half:] = es[:, half:] + lr * hi + li * hr


def _gelu_tanh(x):
    return x * (0.5 * (1.0 + jnp.tanh(math.sqrt(2.0 / math.pi) * (x + 0.044715 * (x * x * x)))))


def _s5_out_body(u_ref, cin_ref, krev_ref, cmat_ref, pr_ref, pi_ref, d_ref, z_ref, ublk, yscr, *, nb):
    half = cin_ref.shape[1] // 2
    _gather_blocks(u_ref, ublk, nb)
    cr, ci = cin_ref[:, :half], cin_ref[:, half:]
    cmat = cmat_ref[...]
    for i in range(S5_BLOCK):
        pr, pi = pr_ref[i:i + 1, :], pi_ref[i:i + 1, :]
        carried = jnp.concatenate([pr * cr - pi * ci, pr * ci + pi * cr], axis=1).astype(BF16)
        y_i = (_dot(ublk[:, :(i + 1) * LANES], krev_ref[(S5_BLOCK - 1 - i) * LANES:, :])
               + _dot(carried, cmat))
        yscr[pl.ds(i, nb, stride=S5_BLOCK), :] = y_i
    y = yscr[...] + d_ref[...] * u_ref[...]
    z_ref[...] = _gelu_tanh(y).astype(z_ref.dtype)


def s5_mixer(u, h0_re, h0_im, n_prompt, prm):
    t, d = u.shape
    g, p = prm["s5_a_re"].shape
    n_cb = d // LANES
    gl = LANES // S5_GROUP_DIM
    L = S5_BLOCK
    nb, nb_prompt = t // L, n_prompt // L
    nbs = nb - nb_prompt
    sw = 2 * gl * p
    krev, wend, cmat, pw_re, pw_im = _s5_tables(
        prm["s5_a_re"], prm["s5_a_im"], prm["s5_log_dt"], prm["s5_b_re"], prm["s5_b_im"],
        prm["s5_c_re"], prm["s5_c_im"], n_cb)
    h0 = jnp.concatenate([h0_re.reshape(nbs, n_cb, gl * p), h0_im.reshape(nbs, n_cb, gl * p)],
                         axis=-1).reshape(nbs, n_cb * sw)
    kw = L * LANES
    ucol = pl.BlockSpec((t, LANES), lambda j: (0, j))
    cin, fin = pl.pallas_call(
        functools.partial(_s5_state_body, nb_prompt=nb_prompt, nb=nb),
        grid=(n_cb,),
        in_specs=[ucol,
                  pl.BlockSpec((None, 2, kw, LANES), lambda j: (j, 0, 0, 0)),
                  pl.BlockSpec((1, sw // 2), lambda j: (0, j)),
                  pl.BlockSpec((1, sw // 2), lambda j: (0, j)),
                  pl.BlockSpec((nbs, sw), lambda j: (0, j))],
        out_specs=[pl.BlockSpec((nb, sw), lambda j: (0, j)),
                   pl.BlockSpec((1 + nbs, sw), lambda j: (0, j))],
        out_shape=[jax.ShapeDtypeStruct((nb, n_cb * sw), F32),
                   jax.ShapeDtypeStruct((1 + nbs, n_cb * sw), F32)],
        scratch_shapes=[pltpu.VMEM((nb, kw), F32), pltpu.VMEM((kw, sw), BF16), pltpu.VMEM((kw, sw), BF16),
                        pltpu.VMEM((nb, sw), F32)],
        compiler_params=_cparams(1, 56),
        name="s5_state",
    )(u, wend, pw_re[L:L + 1], pw_im[L:L + 1], h0)
    z = pl.pallas_call(
        functools.partial(_s5_out_body, nb=nb),
        grid=(n_cb,),
        in_specs=[ucol,
                  pl.BlockSpec((nb, sw), lambda j: (0, j)),
                  pl.BlockSpec((None, kw, LANES), lambda j: (j, 0, 0)),
                  pl.BlockSpec((None, sw, LANES), lambda j: (j, 0, 0)),
                  pl.BlockSpec((L, sw // 2), lambda j: (0, j)),
                  pl.BlockSpec((L, sw // 2), lambda j: (0, j)),
                  pl.BlockSpec((1, LANES), lambda j: (0, j))],
        out_specs=pl.BlockSpec((t, LANES), lambda j: (0, j)),
        out_shape=jax.ShapeDtypeStruct((t, d), BF16),
        scratch_shapes=[pltpu.VMEM((nb, kw), BF16), pltpu.VMEM((t, LANES), F32)],
        compiler_params=_cparams(1, 56),
        name="s5_out",
    )(u, cin, krev, cmat, pw_re[1:], pw_im[1:], prm["s5_d"].reshape(1, d))
    fin = fin.reshape(1 + nbs, n_cb, 2, gl * p)
    fin_re = fin[:, :, 0].reshape(1 + nbs, g, p)
    fin_im = fin[:, :, 1].reshape(1 + nbs, g, p)
    return z, fin_re[:1], fin_im[:1], fin_re[1:], fin_im[1:]


def _cumsum_body(x_ref, init_ref, o_ref, *, n_rows):
    def step(t, carry):
        carry = carry + x_ref[pl.ds(t, 1), :]
        o_ref[pl.ds(t, 1), :] = carry
        return carry

    lax.fori_loop(0, n_rows, step, init_ref[...], unroll=8)


def cumsum_rows(x, init):
    r, c = x.shape
    return pl.pallas_call(
        functools.partial(_cumsum_body, n_rows=r),
        out_shape=jax.ShapeDtypeStruct((r, c), F32),
        compiler_params=pltpu.CompilerParams(vmem_limit_bytes=40 << 20),
        name="cumsum_rows",
    )(x, init)


def _nt_dot(a, b):
    return lax.dot_general(a, b, (((1,), (1,)), ((), ())), preferred_element_type=F32)


def _online_softmax_update(h, s, v, m_scr, l_scr, acc_scr):
    m_prev = m_scr[h]
    m_new = jnp.maximum(m_prev, jnp.max(s, axis=-1, keepdims=True))
    alpha = jnp.exp(m_prev - m_new)
    width = s.shape[1]
    m_wide = jnp.tile(m_new, (1, width // LANES)) if width >= LANES else m_new[:, :width]
    p = jnp.exp(s - m_wide)
    l_scr[h] = alpha * l_scr[h] + jnp.sum(p, axis=-1, keepdims=True)
    acc_scr[h] = alpha * acc_scr[h] + _dot(p.astype(BF16), v)
    m_scr[h] = m_new


def _init_softmax_state(m_scr, l_scr, acc_scr):
    m_scr[...] = jnp.full_like(m_scr, NEG_INF)
    l_scr[...] = jnp.zeros_like(l_scr)
    acc_scr[...] = jnp.zeros_like(acc_scr)


def _causal(s):
    row = lax.broadcasted_iota(jnp.int32, s.shape, 0)
    col = lax.broadcasted_iota(jnp.int32, s.shape, 1)
    return jnp.where(col <= row, s, NEG_INF)


def _fox_prompt_body(q_ref, k_ref, v_ref, cq_ref, ck_ref, o_ref, qa_scr, m_scr, l_scr, acc_scr, *, hb):
    qi = pl.program_id(1)
    ki = pl.program_id(2)
    blk = q_ref.shape[0]

    @pl.when(ki == 0)
    def _():
        _init_softmax_state(m_scr, l_scr, acc_scr)
        lane = lax.broadcasted_iota(jnp.int32, (blk, HEAD_DIM), 1)
        for h in range(hb):
            c = jnp.broadcast_to(cq_ref[h], (blk, HEAD_DIM))
            c0 = c.astype(BF16).astype(F32)
            c1 = (c - c0).astype(BF16).astype(F32)
            c2 = (c - c0 - c1).astype(BF16).astype(F32)
            aug = jnp.where(lane == 0, c0, jnp.where(lane == 1, c1, jnp.where(lane == 2, c2, 0.0)))
            qa_scr[h, :, :HEAD_DIM] = q_ref[:, h * HEAD_DIM:(h + 1) * HEAD_DIM]
            qa_scr[h, :, HEAD_DIM:] = aug.astype(BF16)

    def process(masked):
        ones = jnp.ones((blk, HEAD_DIM), BF16)
        for h in range(hb):
            cols = slice(h * HEAD_DIM, (h + 1) * HEAD_DIM)
            k_aug = jnp.concatenate([k_ref[:, cols], ones], axis=1)
            s = _nt_dot(qa_scr[h], k_aug) - ck_ref[h]
            if masked:
                s = _causal(s)
            _online_softmax_update(h, s, v_ref[:, cols], m_scr, l_scr, acc_scr)

    @pl.when(ki < qi)
    def _():
        process(False)

    @pl.when(ki == qi)
    def _():
        process(True)

    @pl.when(ki == pl.num_programs(2) - 1)
    def _():
        for h in range(hb):
            o_ref[:, h * HEAD_DIM:(h + 1) * HEAD_DIM] = (acc_scr[h] / l_scr[h]).astype(o_ref.dtype)


def fox_prompt_attention(q, k, v, c, n_prompt, blk=512, hb=4):
    d = q.shape[1]
    n_h = d // HEAD_DIM
    blk = _pick_tile(n_prompt, blk, LANES)
    hb = min(hb, n_h)
    nq = n_prompt // blk
    cq = c.T.reshape(n_h, n_prompt, 1)
    ck = c.T.reshape(n_h, 1, n_prompt)
    kv_spec = pl.BlockSpec((blk, hb * HEAD_DIM), lambda h, qi, ki: (jnp.minimum(ki, qi), h))
    return pl.pallas_call(
        functools.partial(_fox_prompt_body, hb=hb),
        grid=(n_h // hb, nq, nq),
        in_specs=[pl.BlockSpec((blk, hb * HEAD_DIM), lambda h, qi, ki: (qi, h)),
                  kv_spec, kv_spec,
                  pl.BlockSpec((hb, blk, 1), lambda h, qi, ki: (h, qi, 0)),
                  pl.BlockSpec((hb, 1, blk), lambda h, qi, ki: (h, 0, jnp.minimum(ki, qi)))],
        out_specs=pl.BlockSpec((blk, hb * HEAD_DIM), lambda h, qi, ki: (qi, h)),
        out_shape=jax.ShapeDtypeStruct((n_prompt, d), BF16),
        scratch_shapes=[pltpu.VMEM((hb, blk, 2 * HEAD_DIM), BF16),
                        pltpu.VMEM((hb, blk, LANES), F32), pltpu.VMEM((hb, blk, LANES), F32),
                        pltpu.VMEM((hb, blk, HEAD_DIM), F32)],
        compiler_params=_cparams(3, 48),
        name="fox_prompt",
    )(q, k, v, cq, ck)


def _fox_sample_body(q_ref, k_ref, v_ref, cq_ref, ckn_ref, ckp_ref, kc_hbm, vc_hbm, o_ref, kbuf, vbuf, sem, *, n_h):
    b = pl.program_id(0)
    h = pl.program_id(1)
    n = b * n_h + h
    slot = n % 2

    def cache_copies(bb, hh, sl):
        return (pltpu.make_async_copy(kc_hbm.at[bb, :, hh, :], kbuf.at[sl], sem.at[0, sl]),
                pltpu.make_async_copy(vc_hbm.at[bb, :, hh, :], vbuf.at[sl], sem.at[1, sl]))

    @pl.when(n == 0)
    def _():
        for cp in cache_copies(b, h, slot):
            cp.start()

    @pl.when(n + 1 < pl.num_programs(0) * n_h)
    def _():
        nxt = n + 1
        for cp in cache_copies(nxt // n_h, nxt % n_h, 1 - slot):
            cp.start()

    for cp in cache_copies(b, h, slot):
        cp.wait()

    q = q_ref[...]
    cq = cq_ref[...]
    s_past = _nt_dot(q, kbuf[slot].astype(BF16)) + cq - ckp_ref[...]
    s_new = _causal(_nt_dot(q, k_ref[...]) + cq - ckn_ref[...])
    m = jnp.maximum(jnp.max(s_past, axis=-1, keepdims=True), jnp.max(s_new, axis=-1, keepdims=True))
    p_past = jnp.exp(s_past - m)
    p_new = jnp.exp(s_new - m)
    l = jnp.sum(p_past, axis=-1, keepdims=True) + jnp.sum(p_new, axis=-1, keepdims=True)
    o = _dot(p_past.astype(BF16), vbuf[slot].astype(BF16)) + _dot(p_new.astype(BF16), v_ref[...])
    o_ref[...] = (o / l).astype(o_ref.dtype)


def fox_sample_attention(q, k, v, k_cache, v_cache, c_past, c_new, n_prompt):
    bsz, past, n_h, _ = k_cache.shape
    d = n_h * HEAD_DIM
    seq = c_new.shape[-1]
    row0 = n_prompt // seq
    new_spec = pl.BlockSpec((seq, HEAD_DIM), lambda b, h: (row0 + b, h))
    return pl.pallas_call(
        functools.partial(_fox_sample_body, n_h=n_h),
        grid=(bsz, n_h),
        in_specs=[new_spec, new_spec, new_spec,
                  pl.BlockSpec((None, None, seq, 1), lambda b, h: (b, h, 0, 0)),
                  pl.BlockSpec((None, None, 1, seq), lambda b, h: (b, h, 0, 0)),
                  pl.BlockSpec((None, None, 1, past), lambda b, h: (b, h, 0, 0)),
                  pl.BlockSpec(memory_space=pl.ANY), pl.BlockSpec(memory_space=pl.ANY)],
        out_specs=pl.BlockSpec((seq, HEAD_DIM), lambda b, h: (b, h)),
        out_shape=jax.ShapeDtypeStruct((bsz * seq, d), BF16),
        scratch_shapes=[pltpu.VMEM((2, past, HEAD_DIM), F32), pltpu.VMEM((2, past, HEAD_DIM), F32),
                        pltpu.SemaphoreType.DMA((2, 2))],
        compiler_params=_cparams(2, 32),
        name="fox_sample",
    )(q, k, v, c_new[..., None], c_new[:, :, None, :], c_past, k_cache, v_cache)


def _prefetched_rows(step_copies, rows):
    i = pl.program_id(0)
    slot = i % 2

    def start(step, sl):
        def body(r, c):
            for cp in step_copies(step, r, sl):
                cp.start()
            return c
        lax.fori_loop(0, rows, body, 0, unroll=8)

    @pl.when(i == 0)
    def _():
        start(i, slot)

    @pl.when(i + 1 < pl.num_programs(0))
    def _():
        start(i + 1, 1 - slot)

    def wait(r, c):
        for cp in step_copies(i, r, slot):
            cp.wait()
        return c

    lax.fori_loop(0, rows, wait, 0, unroll=8)
    return slot


def _gather_body(idx_ref, x_hbm, o_ref, buf, sem):
    rows = o_ref.shape[0]

    def step_copies(step, r, sl):
        src = x_hbm.at[pl.ds(idx_ref[step * rows + r], 1), :]
        return (pltpu.make_async_copy(src, buf.at[sl, pl.ds(r, 1), :], sem.at[sl]),)

    slot = _prefetched_rows(step_copies, rows)
    o_ref[...] = buf[slot].astype(o_ref.dtype)


def gather_rows(x, idx, out_dtype, rows_per_step):
    n_out = idx.shape[0]
    d = x.shape[1]
    assert n_out % rows_per_step == 0 and rows_per_step % 16 == 0
    return pl.pallas_call(
        _gather_body,
        grid_spec=pltpu.PrefetchScalarGridSpec(
            num_scalar_prefetch=1,
            grid=(n_out // rows_per_step,),
            in_specs=[pl.BlockSpec(memory_space=pl.ANY)],
            out_specs=pl.BlockSpec((rows_per_step, d), lambda i, idx: (i, 0)),
            scratch_shapes=[pltpu.VMEM((2, rows_per_step, d), F32), pltpu.SemaphoreType.DMA((2,))],
        ),
        out_shape=jax.ShapeDtypeStruct((n_out, d), out_dtype),
        compiler_params=_cparams(1, 32),
        name="gather_rows",
    )(idx, x)


COMBINE_ROWS = 128


def _combine_body(pos_ref, x_ref, g_ref, y_hbm, o_ref, buf, sem):
    def step_copies(step, r, sl):
        return tuple(
            pltpu.make_async_copy(y_hbm.at[pl.ds(pos_ref[2 * (step * COMBINE_ROWS + r) + k], 1), :],
                                  buf.at[sl, k, pl.ds(r, 1), :], sem.at[sl])
            for k in range(2))

    slot = _prefetched_rows(step_copies, COMBINE_ROWS)
    o_ref[...] = _rms(x_ref[...] + (buf[slot, 0] + buf[slot, 1]), g_ref[...])


def combine_norm(x, y_sorted, pos, g):
    t, d = x.shape
    assert t % COMBINE_ROWS == 0
    return pl.pallas_call(
        _combine_body,
        grid_spec=pltpu.PrefetchScalarGridSpec(
            num_scalar_prefetch=1,
            grid=(t // COMBINE_ROWS,),
            in_specs=[pl.BlockSpec((COMBINE_ROWS, d), lambda i, pos: (i, 0)),
                      pl.BlockSpec((1, d), lambda i, pos: (0, 0)),
                      pl.BlockSpec(memory_space=pl.ANY)],
            out_specs=pl.BlockSpec((COMBINE_ROWS, d), lambda i, pos: (i, 0)),
            scratch_shapes=[pltpu.VMEM((2, 2, COMBINE_ROWS, d), F32), pltpu.SemaphoreType.DMA((2,))],
        ),
        out_shape=jax.ShapeDtypeStruct((t, d), F32),
        compiler_params=_cparams(1, 32),
        name="combine_norm",
    )(pos, x, g.reshape(1, d), y_sorted)


def _route(top_idx, top_gate, n_experts, tm):
    t = top_idx.shape[0]
    n_items = t * top_idx.shape[1]
    n_tiles = (n_items + n_experts * (tm - 1)) // tm
    flat_e = top_idx.reshape(-1)
    order = jnp.argsort(flat_e, stable=True)
    counts = jnp.zeros((n_experts,), jnp.int32).at[flat_e].add(1)
    tiles_e = (counts + tm - 1) // tm
    tile_start = jnp.cumsum(tiles_e) - tiles_e
    item_start = jnp.cumsum(counts) - counts
    e_sorted = flat_e[order]
    dest_sorted = tile_start[e_sorted] * tm + (jnp.arange(n_items, dtype=jnp.int32) - item_start[e_sorted])
    pos = jnp.zeros((n_items,), jnp.int32).at[order].set(dest_sorted)
    rows = n_tiles * tm
    row_token = jnp.zeros((rows,), jnp.int32).at[pos].set(jnp.arange(n_items, dtype=jnp.int32) // top_idx.shape[1])
    row_gate = jnp.zeros((rows,), F32).at[pos].set(top_gate.reshape(-1)).reshape(rows, 1)
    tile_ids = jnp.arange(n_tiles, dtype=jnp.int32)
    n_used = jnp.sum(tiles_e)
    tile_expert = jnp.sum(tile_ids[:, None] >= (tile_start + tiles_e)[None, :], axis=1).astype(jnp.int32)
    last_e = jnp.max(jnp.where(counts > 0, jnp.arange(n_experts), 0)).astype(jnp.int32)
    tile_expert = jnp.where(tile_ids < n_used, jnp.minimum(tile_expert, n_experts - 1), last_e)
    within = tile_ids - tile_start[tile_expert]
    tile_valid = jnp.where(tile_ids < n_used, jnp.clip(counts[tile_expert] - within * tm, 0, tm), 0).astype(jnp.int32)
    return row_token, row_gate, pos, tile_expert, tile_valid


def kernel(x_prompt, x_sample, state_s5_re, state_s5_im, cache_fox_k, cache_fox_v, cache_fox_logf, norm_mix0, s5_a_re, s5_a_im, s5_log_dt, s5_b_re, s5_b_im, s5_c_re, s5_c_im, s5_d, s5_w_glu_a, s5_w_glu_b, norm_ffn0, ffn_w_gate, ffn_w_up, ffn_w_down, norm_mix1, fox_w_in, fox_b_f, fox_w_o, norm_ffn1, moe_router, moe_w_gate, moe_w_up, moe_w_down, norm_final):
    bp, n_prompt, d = x_prompt.shape
    bs, seq, _ = x_sample.shape
    assert bp == 1 and seq == S5_BLOCK and n_prompt % S5_BLOCK == 0
    n_h = d // HEAD_DIM
    n_e = moe_router.shape[1]
    past = cache_fox_k.shape[1]
    x = jnp.concatenate([x_prompt.reshape(n_prompt, d), x_sample.reshape(bs * seq, d)], axis=0)
    t = x.shape[0]
    s5_prm = dict(s5_a_re=s5_a_re, s5_a_im=s5_a_im, s5_log_dt=s5_log_dt, s5_b_re=s5_b_re, s5_b_im=s5_b_im,
                  s5_c_re=s5_c_re, s5_c_im=s5_c_im, s5_d=s5_d)

    (u0,) = rmsnorm(x, [], norm_mix0, [F32])
    z, s5_re_p, s5_im_p, s5_re_s, s5_im_s = s5_mixer(u0, state_s5_re, state_s5_im, n_prompt, s5_prm)
    (xa,) = matmul(z, [s5_w_glu_a, s5_w_glu_b], 0, d, res=x, glu=True)
    (un0,) = rmsnorm(xa, [], norm_ffn0, [BF16])
    tm_dense = _pick_tile(t, 768, FFN_CHUNK)
    n_dense = t // tm_dense
    f0 = swiglu(un0, ffn_w_gate[None], ffn_w_up[None], ffn_w_down[None],
                jnp.zeros((n_dense,), jnp.int32), jnp.full((n_dense,), tm_dense, jnp.int32), tm_dense)

    x1, un1, logf_pad = rmsnorm_head(xa, [f0], norm_mix1, fox_w_in[:, 3 * d:], fox_b_f, BF16, "logf", True)
    logf = logf_pad[:, :n_h]
    scale = HEAD_DIM ** -0.5
    (q,) = matmul(un1, [fox_w_in], 0, d, scale=scale, out_kinds=("bf16",))
    k_f, k_b = matmul(un1, [fox_w_in], d, d, out_kinds=("f32", "bf16"))
    v_f, v_b = matmul(un1, [fox_w_in], 2 * d, d, out_kinds=("f32", "bf16"))

    c_prompt = cumsum_rows(logf[:n_prompt], jnp.zeros((1, n_h), F32))
    o_p = fox_prompt_attention(q, k_b, v_b, c_prompt, n_prompt)
    lc = cache_fox_logf.astype(F32).transpose(1, 0, 2).reshape(past, bs * n_h)
    c_past = cumsum_rows(lc, jnp.zeros((1, bs * n_h), F32))
    ls = logf[n_prompt:].reshape(bs, seq, n_h).transpose(1, 0, 2).reshape(seq, bs * n_h)
    c_new = cumsum_rows(ls, c_past[past - 1:past])
    c_past_b = c_past.reshape(past, bs, n_h).transpose(1, 2, 0).reshape(bs, n_h, 1, past)
    c_new_b = c_new.reshape(seq, bs, n_h).transpose(1, 2, 0)
    o_s = fox_sample_attention(q, k_b, v_b, cache_fox_k, cache_fox_v, c_past_b, c_new_b, n_prompt)
    o = jnp.concatenate([o_p, o_s], axis=0)
    (xb,) = matmul(o, [fox_w_o], 0, d, res=x1)

    un2, route = rmsnorm_head(xb, [], norm_ffn1, moe_router, jnp.zeros((n_e,), F32), F32, "router", False)
    top_idx = route[:, :2].astype(jnp.int32)
    top_gate = route[:, 2:4]
    tm_moe = _moe_tile(t, n_e)
    row_token, row_gate, pos, tile_expert, tile_valid = _route(top_idx, top_gate, n_e, tm_moe)
    x_sorted = gather_rows(un2, row_token, BF16, tm_moe // 4)
    y_sorted = swiglu(x_sorted, moe_w_gate, moe_w_up, moe_w_down, tile_expert, tile_valid, tm_moe, row_gate=row_gate)
    y = combine_norm(xb, y_sorted, pos, norm_final)

    k4 = lambda a, b: a.reshape(b, -1, n_h, HEAD_DIM)
    return (y[:n_prompt].reshape(bp, n_prompt, d), y[n_prompt:].reshape(bs, seq, d),
            s5_re_p, s5_im_p, s5_re_s, s5_im_s,
            k4(k_f[:n_prompt], bp), k4(v_f[:n_prompt], bp), logf[:n_prompt].reshape(bp, n_prompt, n_h),
            k4(k_f[n_prompt:], bs), k4(v_f[n_prompt:], bs), logf[n_prompt:].reshape(bs, seq, n_h))
```

```python
import functools
import math

import jax
import jax.numpy as jnp
from jax import lax
from jax.experimental import pallas as pl
from jax.experimental.pallas import tpu as pltpu

F32 = jnp.float32
BF16 = jnp.bfloat16

RMS_EPS = 1e-5
NEG_INF = -1e30
LOG2_E = math.log2(math.e)
S5_GROUP_DIM = 16
S5_STATE = 64
S5_BLOCK = 16
HEAD_DIM = 128
LANES = 128
V7X_VMEM_BYTES = 64 << 20


def _cparams(n_grid, vmem_mb):
    return pltpu.CompilerParams(dimension_semantics=("arbitrary",) * n_grid,
                                vmem_limit_bytes=min(vmem_mb << 20, V7X_VMEM_BYTES - (2 << 20)))


def _pick_tile(n, cap, mult):
    for t in range(min(cap, n), 0, -1):
        if n % t == 0 and t % mult == 0:
            return t
    raise ValueError(f"no tile for {n} (cap {cap}, multiple of {mult})")


def _split_bf16(x):
    hi = x.astype(BF16)
    lo = (x - hi.astype(F32)).astype(BF16)
    return hi, lo


def _dot(a, b):
    return jnp.dot(a, b, preferred_element_type=F32)


def _dot3(a_hi, a_lo, b_hi, b_lo):
    return _dot(a_hi, b_hi) + _dot(a_lo, b_hi) + _dot(a_hi, b_lo)


def _rms(x, g):
    y = x * lax.rsqrt(jnp.mean(x * x, axis=-1, keepdims=True) + RMS_EPS)
    return y * g


def _norm_body(*refs, n_add, out_sum, n_out):
    x = refs[0][...]
    for r in refs[1:1 + n_add]:
        x = x + r[...]
    g_ref = refs[1 + n_add]
    outs = refs[2 + n_add:]
    if out_sum:
        outs[0][...] = x
        outs = outs[1:]
    y = _rms(x, g_ref[...])
    for o in outs:
        o[...] = y.astype(o.dtype)


def rmsnorm(x, adds, g, out_dtypes, out_sum=False):
    t, d = x.shape
    tm = _pick_tile(t, 256, 8)
    row = pl.BlockSpec((tm, d), lambda i: (i, 0))
    shapes = ([jax.ShapeDtypeStruct((t, d), F32)] if out_sum else []) + [
        jax.ShapeDtypeStruct((t, d), dt) for dt in out_dtypes]
    return pl.pallas_call(
        functools.partial(_norm_body, n_add=len(adds), out_sum=out_sum, n_out=len(out_dtypes)),
        grid=(t // tm,),
        in_specs=[row] * (1 + len(adds)) + [pl.BlockSpec((1, d), lambda i: (0, 0))],
        out_specs=[row] * len(shapes),
        out_shape=shapes,
        compiler_params=_cparams(1, 48),
        name="rmsnorm",
    )(x, *adds, g.reshape(1, d))


def _log_sigmoid(x):
    return jnp.minimum(x, 0.0) - jnp.log1p(jnp.exp(-jnp.abs(x)))


def _top2(logits, n_valid):
    lane = lax.broadcasted_iota(jnp.int32, logits.shape, 1).astype(F32)
    lg = jnp.where(lane < n_valid, logits, NEG_INF)
    m1 = jnp.max(lg, axis=-1, keepdims=True)
    i1 = jnp.min(jnp.where(lg == m1, lane, float(LANES)), axis=-1, keepdims=True)
    lg2 = jnp.where(lane == i1, NEG_INF, lg)
    m2 = jnp.max(lg2, axis=-1, keepdims=True)
    i2 = jnp.min(jnp.where(lg2 == m2, lane, float(LANES)), axis=-1, keepdims=True)
    e2 = jnp.exp(m2 - m1)
    g1 = 1.0 / (1.0 + e2)
    g2 = e2 / (1.0 + e2)
    return jnp.where(lane == 0, i1, jnp.where(lane == 1, i2, jnp.where(lane == 2, g1, jnp.where(lane == 3, g2, 0.0))))


def _norm_head_body(*refs, n_add, out_sum, mode, n_valid):
    x = refs[0][...]
    for r in refs[1:1 + n_add]:
        x = x + r[...]
    g_ref, whi_ref, wlo_ref, b_ref = refs[1 + n_add:5 + n_add]
    outs = refs[5 + n_add:]
    if out_sum:
        outs[0][...] = x
        outs = outs[1:]
    xn_ref, head_ref = outs
    y = _rms(x, g_ref[...])
    xn_ref[...] = y.astype(xn_ref.dtype)
    y_hi, y_lo = _split_bf16(y)
    acc = _dot3(y_hi, y_lo, whi_ref[...], wlo_ref[...]) + b_ref[...]
    if mode == "logf":
        head_ref[...] = _log_sigmoid(acc)
    else:
        head_ref[...] = _top2(acc, n_valid)


def rmsnorm_head(x, adds, g, w_head, b_head, xn_dtype, mode, out_sum):
    t, d = x.shape
    n = w_head.shape[1]
    tm = _pick_tile(t, 256, 8)
    w_pad = jnp.pad(w_head.astype(F32), ((0, 0), (0, LANES - n)))
    w_hi, w_lo = _split_bf16(w_pad)
    b_pad = jnp.pad(b_head.astype(F32), (0, LANES - n)).reshape(1, LANES)
    row = pl.BlockSpec((tm, d), lambda i: (i, 0))
    full = lambda shape: pl.BlockSpec(shape, lambda i: (0, 0))
    shapes = ([jax.ShapeDtypeStruct((t, d), F32)] if out_sum else []) + [
        jax.ShapeDtypeStruct((t, d), xn_dtype), jax.ShapeDtypeStruct((t, LANES), F32)]
    out_specs = ([row] if out_sum else []) + [row, pl.BlockSpec((tm, LANES), lambda i: (i, 0))]
    return pl.pallas_call(
        functools.partial(_norm_head_body, n_add=len(adds), out_sum=out_sum, mode=mode, n_valid=n),
        grid=(t // tm,),
        in_specs=[row] * (1 + len(adds)) + [full((1, d)), full((d, LANES)), full((d, LANES)), full((1, LANES))],
        out_specs=out_specs,
        out_shape=shapes,
        compiler_params=_cparams(1, 48),
        name="rmsnorm_" + mode,
    )(x, *adds, g.reshape(1, d), w_hi, w_lo, b_pad)


def _mm_body(*refs, n_w, has_res, glu, scale, out_kinds):
    x_ref = refs[0]
    w_refs = refs[1:1 + n_w]
    pos = 1 + n_w
    res_ref = refs[pos] if has_res else None
    pos += int(has_res)
    out_refs = refs[pos:pos + len(out_kinds)]
    wbf_refs = refs[pos + len(out_kinds):]

    @pl.when(pl.program_id(1) == 0)
    def _():
        for w_ref, wbf in zip(w_refs, wbf_refs):
            wbf[...] = w_ref[...].astype(BF16)

    x = x_ref[...]
    acc = _dot(x, wbf_refs[0][...])
    if glu:
        acc = acc * jax.nn.sigmoid(_dot(x, wbf_refs[1][...]))
    if has_res:
        acc = acc + res_ref[...]
    for o, kind in zip(out_refs, out_kinds):
        o[...] = acc if kind == "f32" else (acc * scale).astype(BF16)


def matmul(x, ws, col_off, n, res=None, glu=False, scale=1.0, out_kinds=("f32",)):
    t, k = x.shape
    tm = _pick_tile(t, 1056, 16)
    tn = 512 // len(ws)
    assert n % tn == 0 and col_off % tn == 0
    off = col_off // tn
    in_specs = [pl.BlockSpec((tm, k), lambda j, i: (i, 0))]
    in_specs += [pl.BlockSpec((k, tn), lambda j, i: (0, j + off))] * len(ws)
    args = [x, *ws]
    if res is not None:
        in_specs.append(pl.BlockSpec((tm, tn), lambda j, i: (i, j)))
        args.append(res)
    out_tile = pl.BlockSpec((tm, tn), lambda j, i: (i, j))
    shapes = [jax.ShapeDtypeStruct((t, n), F32 if kind == "f32" else BF16) for kind in out_kinds]
    outs = pl.pallas_call(
        functools.partial(_mm_body, n_w=len(ws), has_res=res is not None, glu=glu, scale=scale,
                          out_kinds=out_kinds),
        grid=(n // tn, t // tm),
        in_specs=in_specs,
        out_specs=[out_tile] * len(shapes),
        out_shape=shapes,
        scratch_shapes=[pltpu.VMEM((k, tn), BF16) for _ in ws],
        compiler_params=_cparams(2, 56),
        name="glu" if glu else "matmul",
    )(*args)
    return outs


FFN_CHUNK = 256
TOP_K = 2


def _moe_tile(t, n_experts):
    return -(-int(t * TOP_K / n_experts / 2 * 1.09) // 64) * 64


def _ffn_body(te_ref, tv_ref, x_ref, wg_ref, wu_ref, wd_ref, *rest, gated):
    del te_ref
    if gated:
        gate_ref, o_ref = rest
    else:
        (o_ref,) = rest
    i = pl.program_id(0)
    j = pl.program_id(1)
    valid = tv_ref[i]

    @pl.when(j == 0)
    def _():
        o_ref[...] = jnp.zeros_like(o_ref)

    def weights():
        return wg_ref[...].astype(BF16), wu_ref[...].astype(BF16), wd_ref[...].astype(BF16)

    def accumulate(rows, wg, wu, wd):
        xs = x_ref[rows, :]
        g = _dot(xs, wg)
        u = _dot(xs, wu)
        h = (g * jax.nn.sigmoid(g) * u).astype(BF16)
        o_ref[rows, :] += _dot(h, wd)

    tm = x_ref.shape[0]
    for stop in range(FFN_CHUNK, tm + FFN_CHUNK, FFN_CHUNK):
        @pl.when((valid > stop - FFN_CHUNK) & (valid <= stop))
        def _():
            accumulate(pl.ds(0, min(stop, tm)), *weights())

    if gated:
        @pl.when(j == pl.num_programs(1) - 1)
        def _():
            o_ref[...] = o_ref[...] * gate_ref[...]


def swiglu(x, w_gate, w_up, w_down, tile_expert, tile_valid, tm, row_gate=None, tf=256):
    rows, d = x.shape
    n_e, _, f = w_gate.shape
    n_tiles = rows // tm
    n_j = f // tf
    assert rows % tm == 0 and f % tf == 0 and tm % 16 == 0

    def jeff(i, j, tv):
        return jnp.where(tv[i] > 0, j, n_j - 1)

    once = pl.Buffered(1)
    in_specs = [
        pl.BlockSpec((tm, d), lambda i, j, te, tv: (i, 0), pipeline_mode=once),
        pl.BlockSpec((None, d, tf), lambda i, j, te, tv: (te[i], 0, jeff(i, j, tv))),
        pl.BlockSpec((None, d, tf), lambda i, j, te, tv: (te[i], 0, jeff(i, j, tv))),
        pl.BlockSpec((None, tf, d), lambda i, j, te, tv: (te[i], jeff(i, j, tv), 0)),
    ]
    args = [x, w_gate, w_up, w_down]
    if row_gate is not None:
        in_specs.append(pl.BlockSpec((tm, 1), lambda i, j, te, tv: (i, 0)))
        args.append(row_gate)
    return pl.pallas_call(
        functools.partial(_ffn_body, gated=row_gate is not None),
        grid_spec=pltpu.PrefetchScalarGridSpec(
            num_scalar_prefetch=2,
            grid=(n_tiles, n_j),
            in_specs=in_specs,
            out_specs=pl.BlockSpec((tm, d), lambda i, j, te, tv: (i, 0), pipeline_mode=once),
        ),
        out_shape=jax.ShapeDtypeStruct((rows, d), F32),
        compiler_params=_cparams(2, 62),
        name="swiglu",
    )(tile_expert, tile_valid, *args)


def _s5_tables(a_re, a_im, log_dt, b_re, b_im, c_re, c_im, n_cb):
    hp = lax.Precision.HIGHEST
    g, p = a_re.shape
    c = S5_GROUP_DIM
    gl = LANES // c
    L = S5_BLOCK
    dt = jnp.exp(log_dt)[:, None]
    mag = jnp.exp(a_re * dt)
    lb_re, lb_im = mag * jnp.cos(a_im * dt), mag * jnp.sin(a_im * dt)
    den = a_re * a_re + a_im * a_im
    q_re = ((lb_re - 1.0) * a_re + lb_im * a_im) / den
    q_im = (lb_im * a_re - (lb_re - 1.0) * a_im) / den
    bb_re = q_re[..., None] * b_re - q_im[..., None] * b_im
    bb_im = q_re[..., None] * b_im + q_im[..., None] * b_re
    pw_re, pw_im = [jnp.ones_like(lb_re)], [jnp.zeros_like(lb_re)]
    for _ in range(L):
        r, i = pw_re[-1], pw_im[-1]
        pw_re.append(r * lb_re - i * lb_im)
        pw_im.append(r * lb_im + i * lb_re)
    pw_re, pw_im = jnp.stack(pw_re), jnp.stack(pw_im)
    w_re = pw_re[:L, :, :, None] * bb_re[None] - pw_im[:L, :, :, None] * bb_im[None]
    w_im = pw_re[:L, :, :, None] * bb_im[None] + pw_im[:L, :, :, None] * bb_re[None]
    eye = jnp.eye(gl, dtype=F32)

    taps = (jnp.einsum("gcp,dgpk->dgkc", c_re, w_re, precision=hp)
            - jnp.einsum("gcp,dgpk->dgkc", c_im, w_im, precision=hp))
    krev = taps[::-1].reshape(L, n_cb, gl, c, c)
    krev = jnp.einsum("rjgkc,gh->jrgkhc", krev, eye).reshape(n_cb, L * LANES, LANES).astype(BF16)

    assert 2 * p == LANES
    wend = jnp.stack([w_re[::-1], w_im[::-1]], axis=0).reshape(2, L, n_cb, gl // 2, 2, p, c)
    wend = jnp.transpose(wend, (2, 0, 1, 3, 6, 4, 5)).reshape(n_cb, 2, L * LANES // 2, LANES)

    cmat = jnp.stack([c_re, -c_im], axis=0).reshape(2, n_cb, gl, c, p)
    cmat = jnp.einsum("rjgcp,gh->jrgphc", cmat, eye).reshape(n_cb, 2 * gl * p, LANES).astype(BF16)

    return krev, wend, cmat, pw_re.reshape(L + 1, g * p), pw_im.reshape(L + 1, g * p)


def _gather_blocks(u_ref, dst_ref, nb):
    for s in range(S5_BLOCK):
        dst_ref[:, s * LANES:(s + 1) * LANES] = u_ref[pl.ds(s, nb, stride=S5_BLOCK), :].astype(dst_ref.dtype)


def _s5_state_body(u_ref, w_ref, lr_ref, li_ref, h0_ref, cin_ref, fin_ref, ublk, whi, wlo, e_scr,
                   *, nb_prompt, nb):
    half = cin_ref.shape[1] // 2
    n_pair = half // LANES
    n_grp = LANES // S5_GROUP_DIM
    lane_par = lax.broadcasted_iota(jnp.int32, (S5_BLOCK, S5_GROUP_DIM, LANES), 2) // S5_STATE
    for r in range(2):
        w4 = w_ref[r].reshape(S5_BLOCK, n_pair, S5_GROUP_DIM, LANES)
        for hp in range(n_pair):
            pieces = [jnp.zeros((S5_BLOCK, S5_GROUP_DIM, LANES), F32)] * n_grp
            pieces[2 * hp] = jnp.where(lane_par == 0, w4[:, hp], 0.0)
            pieces[2 * hp + 1] = jnp.where(lane_par == 1, w4[:, hp], 0.0)
            blk = jnp.stack(pieces, axis=1).reshape(S5_BLOCK * LANES, LANES)
            b_hi, b_lo = _split_bf16(blk)
            cols = slice((r * n_pair + hp) * LANES, (r * n_pair + hp + 1) * LANES)
            whi[:, cols] = b_hi
            wlo[:, cols] = b_lo
    _gather_blocks(u_ref, ublk, nb)
    u_hi, u_lo = _split_bf16(ublk[...])
    e_scr[...] = _dot3(u_hi, u_lo, whi[...], wlo[...])
    lr, li = lr_ref[...], li_ref[...]

    def step(m, carry):
        cr, ci = carry
        cin_ref[pl.ds(m, 1), :half] = cr
        cin_ref[pl.ds(m, 1), half:] = ci
        er = e_scr[pl.ds(m, 1), :half]
        ei = e_scr[pl.ds(m, 1), half:]
        return er + lr * cr - li * ci, ei + lr * ci + li * cr

    zero = jnp.zeros((1, half), F32)
    cr, ci = lax.fori_loop(0, nb_prompt, step, (zero, zero))
    fin_ref[0:1, :half] = cr
    fin_ref[0:1, half:] = ci
    h0 = h0_ref[...]
    cin_ref[nb_prompt:nb, :] = h0
    hr, hi = h0[:, :half], h0[:, half:]
    es = e_scr[nb_prompt:nb, :]
    fin_ref[1:1 + nb - nb_prompt, :half] = es[:, :half] + lr * hr - li * hi
    fin_ref[1:1 + nb - nb_prompt, half:] = es[:, half:] + lr * hi + li * hr


def _gelu_tanh(x):
    return x * (0.5 * (1.0 + jnp.tanh(math.sqrt(2.0 / math.pi) * (x + 0.044715 * (x * x * x)))))


def _s5_out_body(u_ref, cin_ref, krev_ref, cmat_ref, pr_ref, pi_ref, d_ref, z_ref, ublk, yscr, *, nb):
    half = cin_ref.shape[1] // 2
    _gather_blocks(u_ref, ublk, nb)
    cr, ci = cin_ref[:, :half], cin_ref[:, half:]
    cmat = cmat_ref[...]
    for i in range(S5_BLOCK):
        pr, pi = pr_ref[i:i + 1, :], pi_ref[i:i + 1, :]
        carried = jnp.concatenate([pr * cr - pi * ci, pr * ci + pi * cr], axis=1).astype(BF16)
        y_i = (_dot(ublk[:, :(i + 1) * LANES], krev_ref[(S5_BLOCK - 1 - i) * LANES:, :])
               + _dot(carried, cmat))
        yscr[pl.ds(i, nb, stride=S5_BLOCK), :] = y_i
    y = yscr[...] + d_ref[...] * u_ref[...]
    z_ref[...] = _gelu_tanh(y).astype(z_ref.dtype)


def s5_mixer(u, h0_re, h0_im, n_prompt, prm):
    t, d = u.shape
    g, p = prm["s5_a_re"].shape
    n_cb = d // LANES
    gl = LANES // S5_GROUP_DIM
    L = S5_BLOCK
    nb, nb_prompt = t // L, n_prompt // L
    nbs = nb - nb_prompt
    sw = 2 * gl * p
    krev, wend, cmat, pw_re, pw_im = _s5_tables(
        prm["s5_a_re"], prm["s5_a_im"], prm["s5_log_dt"], prm["s5_b_re"], prm["s5_b_im"],
        prm["s5_c_re"], prm["s5_c_im"], n_cb)
    h0 = jnp.concatenate([h0_re.reshape(nbs, n_cb, gl * p), h0_im.reshape(nbs, n_cb, gl * p)],
                         axis=-1).reshape(nbs, n_cb * sw)
    kw = L * LANES
    ucol = pl.BlockSpec((t, LANES), lambda j: (0, j))
    cin, fin = pl.pallas_call(
        functools.partial(_s5_state_body, nb_prompt=nb_prompt, nb=nb),
        grid=(n_cb,),
        in_specs=[ucol,
                  pl.BlockSpec((None, 2, kw // 2, LANES), lambda j: (j, 0, 0, 0)),
                  pl.BlockSpec((1, sw // 2), lambda j: (0, j)),
                  pl.BlockSpec((1, sw // 2), lambda j: (0, j)),
                  pl.BlockSpec((nbs, sw), lambda j: (0, j))],
        out_specs=[pl.BlockSpec((nb, sw), lambda j: (0, j)),
                   pl.BlockSpec((1 + nbs, sw), lambda j: (0, j))],
        out_shape=[jax.ShapeDtypeStruct((nb, n_cb * sw), F32),
                   jax.ShapeDtypeStruct((1 + nbs, n_cb * sw), F32)],
        scratch_shapes=[pltpu.VMEM((nb, kw), F32), pltpu.VMEM((kw, sw), BF16), pltpu.VMEM((kw, sw), BF16),
                        pltpu.VMEM((nb, sw), F32)],
        compiler_params=_cparams(1, 56),
        name="s5_state",
    )(u, wend, pw_re[L:L + 1], pw_im[L:L + 1], h0)
    z = pl.pallas_call(
        functools.partial(_s5_out_body, nb=nb),
        grid=(n_cb,),
        in_specs=[ucol,
                  pl.BlockSpec((nb, sw), lambda j: (0, j)),
                  pl.BlockSpec((None, kw, LANES), lambda j: (j, 0, 0)),
                  pl.BlockSpec((None, sw, LANES), lambda j: (j, 0, 0)),
                  pl.BlockSpec((L, sw // 2), lambda j: (0, j)),
                  pl.BlockSpec((L, sw // 2), lambda j: (0, j)),
                  pl.BlockSpec((1, LANES), lambda j: (0, j))],
        out_specs=pl.BlockSpec((t, LANES), lambda j: (0, j)),
        out_shape=jax.ShapeDtypeStruct((t, d), BF16),
        scratch_shapes=[pltpu.VMEM((nb, kw), BF16), pltpu.VMEM((t, LANES), F32)],
        compiler_params=_cparams(1, 56),
        name="s5_out",
    )(u, cin, krev, cmat, pw_re[1:], pw_im[1:], prm["s5_d"].reshape(1, d))
    fin = fin.reshape(1 + nbs, n_cb, 2, gl * p)
    fin_re = fin[:, :, 0].reshape(1 + nbs, g, p)
    fin_im = fin[:, :, 1].reshape(1 + nbs, g, p)
    return z, fin_re[:1], fin_im[:1], fin_re[1:], fin_im[1:]


def _cumsum_body(x_ref, init_ref, o_ref, *, n_rows):
    def step(t, carry):
        carry = carry + x_ref[pl.ds(t, 1), :]
        o_ref[pl.ds(t, 1), :] = carry
        return carry

    lax.fori_loop(0, n_rows, step, init_ref[...], unroll=8)


def cumsum_rows(x, init):
    r, c = x.shape
    return pl.pallas_call(
        functools.partial(_cumsum_body, n_rows=r),
        out_shape=jax.ShapeDtypeStruct((r, c), F32),
        compiler_params=pltpu.CompilerParams(vmem_limit_bytes=40 << 20),
        name="cumsum_rows",
    )(x, init)


def _nt_dot(a, b):
    return lax.dot_general(a, b, (((1,), (1,)), ((), ())), preferred_element_type=F32)


def _online_softmax_update(h, s, v, m_scr, l_scr, acc_scr):
    m_prev = m_scr[h]
    m_new = jnp.maximum(m_prev, jnp.max(s, axis=-1, keepdims=True))
    alpha = jnp.exp2(m_prev - m_new)
    width = s.shape[1]
    m_wide = jnp.tile(m_new, (1, width // LANES)) if width >= LANES else m_new[:, :width]
    p = jnp.exp2(s - m_wide)
    l_scr[h] = alpha * l_scr[h] + jnp.sum(p, axis=-1, keepdims=True)
    acc_scr[h] = alpha * acc_scr[h] + _dot(p.astype(BF16), v)
    m_scr[h] = m_new


def _init_softmax_state(m_scr, l_scr, acc_scr):
    m_scr[...] = jnp.full_like(m_scr, NEG_INF)
    l_scr[...] = jnp.zeros_like(l_scr)
    acc_scr[...] = jnp.zeros_like(acc_scr)


def _causal(s):
    row = lax.broadcasted_iota(jnp.int32, s.shape, 0)
    col = lax.broadcasted_iota(jnp.int32, s.shape, 1)
    return jnp.where(col <= row, s, NEG_INF)


def _fox_prompt_body(q_ref, k_ref, v_ref, cq_ref, ck_ref, o_ref, qa_scr, m_scr, l_scr, acc_scr, *, hb):
    qi = pl.program_id(1)
    ki = pl.program_id(2)
    blk = q_ref.shape[0]

    @pl.when(ki == 0)
    def _():
        _init_softmax_state(m_scr, l_scr, acc_scr)
        lane = lax.broadcasted_iota(jnp.int32, (blk, HEAD_DIM), 1)
        for h in range(hb):
            c = jnp.broadcast_to(cq_ref[h], (blk, HEAD_DIM))
            c0 = c.astype(BF16).astype(F32)
            c1 = (c - c0).astype(BF16).astype(F32)
            c2 = (c - c0 - c1).astype(BF16).astype(F32)
            aug = jnp.where(lane == 0, c0, jnp.where(lane == 1, c1, jnp.where(lane == 2, c2, 0.0)))
            qa_scr[h, :, :HEAD_DIM] = q_ref[:, h * HEAD_DIM:(h + 1) * HEAD_DIM]
            qa_scr[h, :, HEAD_DIM:] = aug.astype(BF16)

    def process(masked):
        ones = jnp.ones((blk, HEAD_DIM), BF16)
        for h in range(hb):
            cols = slice(h * HEAD_DIM, (h + 1) * HEAD_DIM)
            k_aug = jnp.concatenate([k_ref[:, cols], ones], axis=1)
            s = _nt_dot(qa_scr[h], k_aug) - ck_ref[h]
            if masked:
                s = _causal(s)
            _online_softmax_update(h, s, v_ref[:, cols], m_scr, l_scr, acc_scr)

    @pl.when(ki < qi)
    def _():
        process(False)

    @pl.when(ki == qi)
    def _():
        process(True)

    @pl.when(ki == pl.num_programs(2) - 1)
    def _():
        for h in range(hb):
            o_ref[:, h * HEAD_DIM:(h + 1) * HEAD_DIM] = (acc_scr[h] / l_scr[h]).astype(o_ref.dtype)


def fox_prompt_attention(q, k, v, c, n_prompt, blk=1024, hb=4):
    d = q.shape[1]
    n_h = d // HEAD_DIM
    blk = _pick_tile(n_prompt, blk, LANES)
    hb = min(hb, n_h)
    nq = n_prompt // blk
    cq = c.T.reshape(n_h, n_prompt, 1)
    ck = c.T.reshape(n_h, 1, n_prompt)
    kv_spec = pl.BlockSpec((blk, hb * HEAD_DIM), lambda h, qi, ki: (jnp.minimum(ki, qi), h))
    return pl.pallas_call(
        functools.partial(_fox_prompt_body, hb=hb),
        grid=(n_h // hb, nq, nq),
        in_specs=[pl.BlockSpec((blk, hb * HEAD_DIM), lambda h, qi, ki: (qi, h)),
                  kv_spec, kv_spec,
                  pl.BlockSpec((hb, blk, 1), lambda h, qi, ki: (h, qi, 0)),
                  pl.BlockSpec((hb, 1, blk), lambda h, qi, ki: (h, 0, jnp.minimum(ki, qi)))],
        out_specs=pl.BlockSpec((blk, hb * HEAD_DIM), lambda h, qi, ki: (qi, h)),
        out_shape=jax.ShapeDtypeStruct((n_prompt, d), BF16),
        scratch_shapes=[pltpu.VMEM((hb, blk, 2 * HEAD_DIM), BF16),
                        pltpu.VMEM((hb, blk, LANES), F32), pltpu.VMEM((hb, blk, LANES), F32),
                        pltpu.VMEM((hb, blk, HEAD_DIM), F32)],
        compiler_params=_cparams(3, 48),
        name="fox_prompt",
    )(q, k, v, cq, ck)


def _fox_sample_body(q_ref, k_ref, v_ref, cq_ref, ckn_ref, ckp_ref, kc_hbm, vc_hbm, o_ref, kbuf, vbuf, sem, *, n_h):
    b = pl.program_id(0)
    h = pl.program_id(1)
    n = b * n_h + h
    slot = n % 2

    def cache_copies(bb, hh, sl):
        return (pltpu.make_async_copy(kc_hbm.at[bb, :, hh, :], kbuf.at[sl], sem.at[0, sl]),
                pltpu.make_async_copy(vc_hbm.at[bb, :, hh, :], vbuf.at[sl], sem.at[1, sl]))

    @pl.when(n == 0)
    def _():
        for cp in cache_copies(b, h, slot):
            cp.start()

    @pl.when(n + 1 < pl.num_programs(0) * n_h)
    def _():
        nxt = n + 1
        for cp in cache_copies(nxt // n_h, nxt % n_h, 1 - slot):
            cp.start()

    for cp in cache_copies(b, h, slot):
        cp.wait()

    q = q_ref[...]
    cq = cq_ref[...]
    s_past = _nt_dot(q, kbuf[slot].astype(BF16)) + cq - ckp_ref[...]
    s_new = _causal(_nt_dot(q, k_ref[...]) + cq - ckn_ref[...])
    m = jnp.maximum(jnp.max(s_past, axis=-1, keepdims=True), jnp.max(s_new, axis=-1, keepdims=True))
    p_past = jnp.exp2(s_past - m)
    p_new = jnp.exp2(s_new - m)
    l = jnp.sum(p_past, axis=-1, keepdims=True) + jnp.sum(p_new, axis=-1, keepdims=True)
    o = _dot(p_past.astype(BF16), vbuf[slot].astype(BF16)) + _dot(p_new.astype(BF16), v_ref[...])
    o_ref[...] = (o / l).astype(o_ref.dtype)


def fox_sample_attention(q, k, v, k_cache, v_cache, c_past, c_new, n_prompt):
    bsz, past, n_h, _ = k_cache.shape
    d = n_h * HEAD_DIM
    seq = c_new.shape[-1]
    row0 = n_prompt // seq
    new_spec = pl.BlockSpec((seq, HEAD_DIM), lambda b, h: (row0 + b, h))
    return pl.pallas_call(
        functools.partial(_fox_sample_body, n_h=n_h),
        grid=(bsz, n_h),
        in_specs=[new_spec, new_spec, new_spec,
                  pl.BlockSpec((None, None, seq, 1), lambda b, h: (b, h, 0, 0)),
                  pl.BlockSpec((None, None, 1, seq), lambda b, h: (b, h, 0, 0)),
                  pl.BlockSpec((None, None, 1, past), lambda b, h: (b, h, 0, 0)),
                  pl.BlockSpec(memory_space=pl.ANY), pl.BlockSpec(memory_space=pl.ANY)],
        out_specs=pl.BlockSpec((seq, HEAD_DIM), lambda b, h: (b, h)),
        out_shape=jax.ShapeDtypeStruct((bsz * seq, d), BF16),
        scratch_shapes=[pltpu.VMEM((2, past, HEAD_DIM), F32), pltpu.VMEM((2, past, HEAD_DIM), F32),
                        pltpu.SemaphoreType.DMA((2, 2))],
        compiler_params=_cparams(2, 32),
        name="fox_sample",
    )(q, k, v, c_new[..., None], c_new[:, :, None, :], c_past, k_cache, v_cache)


def _prefetched_rows(step_copies, n_rows):
    i = pl.program_id(0)
    slot = i % 2

    def start(step, sl):
        def body(r, c):
            for cp in step_copies(step, r, sl):
                cp.start()
            return c
        lax.fori_loop(0, n_rows(step), body, 0)

    @pl.when(i == 0)
    def _():
        start(i, slot)

    @pl.when(i + 1 < pl.num_programs(0))
    def _():
        start(i + 1, 1 - slot)

    def wait(r, c):
        for cp in step_copies(i, r, slot):
            cp.wait()
        return c

    lax.fori_loop(0, n_rows(i), wait, 0)
    return slot


def _gather_body(idx_ref, nv_ref, x_hbm, o_ref, buf, sem):
    rows = o_ref.shape[0]

    @pl.when(pl.program_id(0) == 0)
    def _():
        buf[...] = jnp.zeros_like(buf)

    def step_copies(step, r, sl):
        src = x_hbm.at[pl.ds(idx_ref[step * rows + r], 1), :]
        return (pltpu.make_async_copy(src, buf.at[sl, pl.ds(r, 1), :], sem.at[sl]),)

    slot = _prefetched_rows(step_copies, lambda step: nv_ref[step])
    live = lax.broadcasted_iota(jnp.int32, (rows, 1), 0) < nv_ref[pl.program_id(0)]
    o_ref[...] = jnp.where(live, buf[slot], 0.0).astype(o_ref.dtype)


def gather_rows(x, idx, n_live, out_dtype, rows_per_step):
    n_out = idx.shape[0]
    d = x.shape[1]
    assert n_out % rows_per_step == 0 and rows_per_step % 16 == 0
    return pl.pallas_call(
        _gather_body,
        grid_spec=pltpu.PrefetchScalarGridSpec(
            num_scalar_prefetch=2,
            grid=(n_out // rows_per_step,),
            in_specs=[pl.BlockSpec(memory_space=pl.ANY)],
            out_specs=pl.BlockSpec((rows_per_step, d), lambda i, idx, nv: (i, 0)),
            scratch_shapes=[pltpu.VMEM((2, rows_per_step, d), F32), pltpu.SemaphoreType.DMA((2,))],
        ),
        out_shape=jax.ShapeDtypeStruct((n_out, d), out_dtype),
        compiler_params=_cparams(1, 32),
        name="gather_rows",
    )(idx, n_live, x)


COMBINE_ROWS = 128


def _combine_body(pos_ref, x_ref, g_ref, y_hbm, o_ref, buf, sem):
    def step_copies(step, r, sl):
        return tuple(
            pltpu.make_async_copy(y_hbm.at[pl.ds(pos_ref[2 * (step * COMBINE_ROWS + r) + k], 1), :],
                                  buf.at[sl, k, pl.ds(r, 1), :], sem.at[sl])
            for k in range(2))

    slot = _prefetched_rows(step_copies, lambda step: COMBINE_ROWS)
    o_ref[...] = _rms(x_ref[...] + (buf[slot, 0] + buf[slot, 1]), g_ref[...])


def combine_norm(x, y_sorted, pos, g):
    t, d = x.shape
    assert t % COMBINE_ROWS == 0
    return pl.pallas_call(
        _combine_body,
        grid_spec=pltpu.PrefetchScalarGridSpec(
            num_scalar_prefetch=1,
            grid=(t // COMBINE_ROWS,),
            in_specs=[pl.BlockSpec((COMBINE_ROWS, d), lambda i, pos: (i, 0)),
                      pl.BlockSpec((1, d), lambda i, pos: (0, 0)),
                      pl.BlockSpec(memory_space=pl.ANY)],
            out_specs=pl.BlockSpec((COMBINE_ROWS, d), lambda i, pos: (i, 0)),
            scratch_shapes=[pltpu.VMEM((2, 2, COMBINE_ROWS, d), F32), pltpu.SemaphoreType.DMA((2,))],
        ),
        out_shape=jax.ShapeDtypeStruct((t, d), F32),
        compiler_params=_cparams(1, 32),
        name="combine_norm",
    )(pos, x, g.reshape(1, d), y_sorted)


def _route(top_idx, top_gate, n_experts, tm, gather_rows_per_step):
    t = top_idx.shape[0]
    n_items = t * top_idx.shape[1]
    n_tiles = (n_items + n_experts * (tm - 1)) // tm
    flat_e = top_idx.reshape(-1)
    order = jnp.argsort(flat_e, stable=True)
    counts = jnp.zeros((n_experts,), jnp.int32).at[flat_e].add(1)
    tiles_e = (counts + tm - 1) // tm
    tile_start = jnp.cumsum(tiles_e) - tiles_e
    item_start = jnp.cumsum(counts) - counts
    e_sorted = flat_e[order]
    dest_sorted = tile_start[e_sorted] * tm + (jnp.arange(n_items, dtype=jnp.int32) - item_start[e_sorted])
    pos = jnp.zeros((n_items,), jnp.int32).at[order].set(dest_sorted)
    rows = n_tiles * tm
    row_token = jnp.zeros((rows,), jnp.int32).at[pos].set(jnp.arange(n_items, dtype=jnp.int32) // top_idx.shape[1])
    row_gate = jnp.zeros((rows,), F32).at[pos].set(top_gate.reshape(-1)).reshape(rows, 1)
    tile_ids = jnp.arange(n_tiles, dtype=jnp.int32)
    n_used = jnp.sum(tiles_e)
    tile_expert = jnp.sum(tile_ids[:, None] >= (tile_start + tiles_e)[None, :], axis=1).astype(jnp.int32)
    last_e = jnp.max(jnp.where(counts > 0, jnp.arange(n_experts), 0)).astype(jnp.int32)
    tile_expert = jnp.where(tile_ids < n_used, jnp.minimum(tile_expert, n_experts - 1), last_e)
    within = tile_ids - tile_start[tile_expert]
    tile_valid = jnp.where(tile_ids < n_used, jnp.clip(counts[tile_expert] - within * tm, 0, tm), 0).astype(jnp.int32)
    steps_per_tile = tm // gather_rows_per_step
    step_off = (jnp.arange(n_tiles * steps_per_tile, dtype=jnp.int32) % steps_per_tile) * gather_rows_per_step
    gather_valid = jnp.clip(jnp.repeat(tile_valid, steps_per_tile) - step_off, 0, gather_rows_per_step)
    return row_token, row_gate, pos, tile_expert, tile_valid, gather_valid.astype(jnp.int32)


def kernel(x_prompt, x_sample, state_s5_re, state_s5_im, cache_fox_k, cache_fox_v, cache_fox_logf, norm_mix0, s5_a_re, s5_a_im, s5_log_dt, s5_b_re, s5_b_im, s5_c_re, s5_c_im, s5_d, s5_w_glu_a, s5_w_glu_b, norm_ffn0, ffn_w_gate, ffn_w_up, ffn_w_down, norm_mix1, fox_w_in, fox_b_f, fox_w_o, norm_ffn1, moe_router, moe_w_gate, moe_w_up, moe_w_down, norm_final):
    bp, n_prompt, d = x_prompt.shape
    bs, seq, _ = x_sample.shape
    assert bp == 1 and seq == S5_BLOCK and n_prompt % S5_BLOCK == 0
    n_h = d // HEAD_DIM
    n_e = moe_router.shape[1]
    past = cache_fox_k.shape[1]
    x = jnp.concatenate([x_prompt.reshape(n_prompt, d), x_sample.reshape(bs * seq, d)], axis=0)
    t = x.shape[0]
    s5_prm = dict(s5_a_re=s5_a_re, s5_a_im=s5_a_im, s5_log_dt=s5_log_dt, s5_b_re=s5_b_re, s5_b_im=s5_b_im,
                  s5_c_re=s5_c_re, s5_c_im=s5_c_im, s5_d=s5_d)

    (u0,) = rmsnorm(x, [], norm_mix0, [F32])
    z, s5_re_p, s5_im_p, s5_re_s, s5_im_s = s5_mixer(u0, state_s5_re, state_s5_im, n_prompt, s5_prm)
    (xa,) = matmul(z, [s5_w_glu_a, s5_w_glu_b], 0, d, res=x, glu=True)
    (un0,) = rmsnorm(xa, [], norm_ffn0, [BF16])
    tm_dense = _pick_tile(t, 1152, 16)
    n_dense = t // tm_dense
    f0 = swiglu(un0, ffn_w_gate[None], ffn_w_up[None], ffn_w_down[None],
                jnp.zeros((n_dense,), jnp.int32), jnp.full((n_dense,), tm_dense, jnp.int32), tm_dense)

    x1, un1, logf_pad = rmsnorm_head(xa, [f0], norm_mix1, fox_w_in[:, 3 * d:], fox_b_f, BF16, "logf", True)
    logf = logf_pad[:, :n_h]
    scale = HEAD_DIM ** -0.5 * LOG2_E
    (q,) = matmul(un1, [fox_w_in], 0, d, scale=scale, out_kinds=("bf16",))
    k_f, k_b = matmul(un1, [fox_w_in], d, d, out_kinds=("f32", "bf16"))
    v_f, v_b = matmul(un1, [fox_w_in], 2 * d, d, out_kinds=("f32", "bf16"))

    c_prompt = cumsum_rows(logf[:n_prompt], jnp.zeros((1, n_h), F32))
    o_p = fox_prompt_attention(q, k_b, v_b, c_prompt * LOG2_E, n_prompt)
    lc = cache_fox_logf.astype(F32).transpose(1, 0, 2).reshape(past, bs * n_h)
    c_past = cumsum_rows(lc, jnp.zeros((1, bs * n_h), F32))
    ls = logf[n_prompt:].reshape(bs, seq, n_h).transpose(1, 0, 2).reshape(seq, bs * n_h)
    c_new = cumsum_rows(ls, c_past[past - 1:past])
    c_past_b = (c_past * LOG2_E).reshape(past, bs, n_h).transpose(1, 2, 0).reshape(bs, n_h, 1, past)
    c_new_b = (c_new * LOG2_E).reshape(seq, bs, n_h).transpose(1, 2, 0)
    o_s = fox_sample_attention(q, k_b, v_b, cache_fox_k, cache_fox_v, c_past_b, c_new_b, n_prompt)
    o = jnp.concatenate([o_p, o_s], axis=0)
    (xb,) = matmul(o, [fox_w_o], 0, d, res=x1)

    un2, route = rmsnorm_head(xb, [], norm_ffn1, moe_router, jnp.zeros((n_e,), F32), F32, "router", False)
    top_idx = route[:, :2].astype(jnp.int32)
    top_gate = route[:, 2:4]
    tm_moe = _moe_tile(t, n_e)
    row_token, row_gate, pos, tile_expert, tile_valid, gather_valid = _route(top_idx, top_gate, n_e, tm_moe, tm_moe // 4)
    x_sorted = gather_rows(un2, row_token, gather_valid, BF16, tm_moe // 4)
    y_sorted = swiglu(x_sorted, moe_w_gate, moe_w_up, moe_w_down, tile_expert, tile_valid, tm_moe, row_gate=row_gate)
    y = combine_norm(xb, y_sorted, pos, norm_final)

    k4 = lambda a, b: a.reshape(b, -1, n_h, HEAD_DIM)
    return (y[:n_prompt].reshape(bp, n_prompt, d), y[n_prompt:].reshape(bs, seq, d),
            s5_re_p, s5_im_p, s5_re_s, s5_im_s,
            k4(k_f[:n_prompt], bp), k4(v_f[:n_prompt], bp), logf[:n_prompt].reshape(bp, n_prompt, n_h),
            k4(k_f[n_prompt:], bs), k4(v_f[n_prompt:], bs), logf[n_prompt:].reshape(bs, seq, n_h))
```

```python
import functools
import math

import jax
import jax.numpy as jnp
from jax import lax
from jax.experimental import pallas as pl
from jax.experimental.pallas import tpu as pltpu

F32 = jnp.float32
BF16 = jnp.bfloat16

RMS_EPS = 1e-5
NEG_INF = -1e30
LOG2_E = math.log2(math.e)
S5_GROUP_DIM = 16
S5_STATE = 64
S5_BLOCK = 16
HEAD_DIM = 128
LANES = 128
V7X_VMEM_BYTES = 64 << 20


def _cparams(n_grid, vmem_mb):
    return pltpu.CompilerParams(dimension_semantics=("arbitrary",) * n_grid,
                                vmem_limit_bytes=min(vmem_mb << 20, V7X_VMEM_BYTES - (2 << 20)))


def _pick_tile(n, cap, mult):
    for t in range(min(cap, n), 0, -1):
        if n % t == 0 and t % mult == 0:
            return t
    raise ValueError(f"no tile for {n} (cap {cap}, multiple of {mult})")


def _split_bf16(x):
    hi = x.astype(BF16)
    lo = (x - hi.astype(F32)).astype(BF16)
    return hi, lo


def _dot(a, b):
    return jnp.dot(a, b, preferred_element_type=F32)


def _dot3(a_hi, a_lo, b_hi, b_lo):
    return _dot(a_hi, b_hi) + _dot(a_lo, b_hi) + _dot(a_hi, b_lo)


def _rms(x, g):
    y = x * lax.rsqrt(jnp.mean(x * x, axis=-1, keepdims=True) + RMS_EPS)
    return y * g


def _norm_body(*refs, n_add, out_sum, n_out):
    x = refs[0][...]
    for r in refs[1:1 + n_add]:
        x = x + r[...]
    g_ref = refs[1 + n_add]
    outs = refs[2 + n_add:]
    if out_sum:
        outs[0][...] = x
        outs = outs[1:]
    y = _rms(x, g_ref[...])
    for o in outs:
        o[...] = y.astype(o.dtype)


def rmsnorm(x, adds, g, out_dtypes, out_sum=False):
    t, d = x.shape
    tm = _pick_tile(t, 256, 8)
    row = pl.BlockSpec((tm, d), lambda i: (i, 0))
    shapes = ([jax.ShapeDtypeStruct((t, d), F32)] if out_sum else []) + [
        jax.ShapeDtypeStruct((t, d), dt) for dt in out_dtypes]
    return pl.pallas_call(
        functools.partial(_norm_body, n_add=len(adds), out_sum=out_sum, n_out=len(out_dtypes)),
        grid=(t // tm,),
        in_specs=[row] * (1 + len(adds)) + [pl.BlockSpec((1, d), lambda i: (0, 0))],
        out_specs=[row] * len(shapes),
        out_shape=shapes,
        compiler_params=_cparams(1, 48),
        name="rmsnorm",
    )(x, *adds, g.reshape(1, d))


def _log_sigmoid(x):
    return jnp.minimum(x, 0.0) - jnp.log1p(jnp.exp(-jnp.abs(x)))


def _top2(logits, n_valid):
    lane = lax.broadcasted_iota(jnp.int32, logits.shape, 1).astype(F32)
    lg = jnp.where(lane < n_valid, logits, NEG_INF)
    m1 = jnp.max(lg, axis=-1, keepdims=True)
    i1 = jnp.min(jnp.where(lg == m1, lane, float(LANES)), axis=-1, keepdims=True)
    lg2 = jnp.where(lane == i1, NEG_INF, lg)
    m2 = jnp.max(lg2, axis=-1, keepdims=True)
    i2 = jnp.min(jnp.where(lg2 == m2, lane, float(LANES)), axis=-1, keepdims=True)
    e2 = jnp.exp(m2 - m1)
    g1 = 1.0 / (1.0 + e2)
    g2 = e2 / (1.0 + e2)
    return jnp.where(lane == 0, i1, jnp.where(lane == 1, i2, jnp.where(lane == 2, g1, jnp.where(lane == 3, g2, 0.0))))


def _norm_head_body(*refs, n_add, out_sum, mode, n_valid):
    x = refs[0][...]
    for r in refs[1:1 + n_add]:
        x = x + r[...]
    g_ref, whi_ref, wlo_ref, b_ref = refs[1 + n_add:5 + n_add]
    outs = refs[5 + n_add:]
    if out_sum:
        outs[0][...] = x
        outs = outs[1:]
    xn_ref, head_ref = outs
    y = _rms(x, g_ref[...])
    xn_ref[...] = y.astype(xn_ref.dtype)
    y_hi, y_lo = _split_bf16(y)
    acc = _dot3(y_hi, y_lo, whi_ref[...], wlo_ref[...]) + b_ref[...]
    if mode == "logf":
        head_ref[...] = _log_sigmoid(acc)
    else:
        head_ref[...] = _top2(acc, n_valid)


def rmsnorm_head(x, adds, g, w_head, b_head, xn_dtype, mode, out_sum):
    t, d = x.shape
    n = w_head.shape[1]
    tm = _pick_tile(t, 256, 8)
    w_pad = jnp.pad(w_head.astype(F32), ((0, 0), (0, LANES - n)))
    w_hi, w_lo = _split_bf16(w_pad)
    b_pad = jnp.pad(b_head.astype(F32), (0, LANES - n)).reshape(1, LANES)
    row = pl.BlockSpec((tm, d), lambda i: (i, 0))
    full = lambda shape: pl.BlockSpec(shape, lambda i: (0, 0))
    shapes = ([jax.ShapeDtypeStruct((t, d), F32)] if out_sum else []) + [
        jax.ShapeDtypeStruct((t, d), xn_dtype), jax.ShapeDtypeStruct((t, LANES), F32)]
    out_specs = ([row] if out_sum else []) + [row, pl.BlockSpec((tm, LANES), lambda i: (i, 0))]
    return pl.pallas_call(
        functools.partial(_norm_head_body, n_add=len(adds), out_sum=out_sum, mode=mode, n_valid=n),
        grid=(t // tm,),
        in_specs=[row] * (1 + len(adds)) + [full((1, d)), full((d, LANES)), full((d, LANES)), full((1, LANES))],
        out_specs=out_specs,
        out_shape=shapes,
        compiler_params=_cparams(1, 48),
        name="rmsnorm_" + mode,
    )(x, *adds, g.reshape(1, d), w_hi, w_lo, b_pad)


def _mm_body(*refs, n_w, has_res, glu, scale, out_kinds):
    x_ref = refs[0]
    w_refs = refs[1:1 + n_w]
    pos = 1 + n_w
    res_ref = refs[pos] if has_res else None
    pos += int(has_res)
    out_refs = refs[pos:pos + len(out_kinds)]
    wbf_refs = refs[pos + len(out_kinds):]

    @pl.when(pl.program_id(1) == 0)
    def _():
        for w_ref, wbf in zip(w_refs, wbf_refs):
            wbf[...] = w_ref[...].astype(BF16)

    x = x_ref[...]
    acc = _dot(x, wbf_refs[0][...])
    if glu:
        acc = acc * jax.nn.sigmoid(_dot(x, wbf_refs[1][...]))
    if has_res:
        acc = acc + res_ref[...]
    for o, kind in zip(out_refs, out_kinds):
        o[...] = acc if kind == "f32" else (acc * scale).astype(BF16)


def matmul(x, ws, col_off, n, res=None, glu=False, scale=1.0, out_kinds=("f32",)):
    t, k = x.shape
    tm = _pick_tile(t, 1056, 16)
    tn = 512 // len(ws)
    assert n % tn == 0 and col_off % tn == 0
    off = col_off // tn
    in_specs = [pl.BlockSpec((tm, k), lambda j, i: (i, 0))]
    in_specs += [pl.BlockSpec((k, tn), lambda j, i: (0, j + off))] * len(ws)
    args = [x, *ws]
    if res is not None:
        in_specs.append(pl.BlockSpec((tm, tn), lambda j, i: (i, j)))
        args.append(res)
    out_tile = pl.BlockSpec((tm, tn), lambda j, i: (i, j))
    shapes = [jax.ShapeDtypeStruct((t, n), F32 if kind == "f32" else BF16) for kind in out_kinds]
    outs = pl.pallas_call(
        functools.partial(_mm_body, n_w=len(ws), has_res=res is not None, glu=glu, scale=scale,
                          out_kinds=out_kinds),
        grid=(n // tn, t // tm),
        in_specs=in_specs,
        out_specs=[out_tile] * len(shapes),
        out_shape=shapes,
        scratch_shapes=[pltpu.VMEM((k, tn), BF16) for _ in ws],
        compiler_params=_cparams(2, 56),
        name="glu" if glu else "matmul",
    )(*args)
    return outs


FFN_CHUNK = 256
TOP_K = 2


def _moe_tile(t, n_experts):
    return -(-int(t * TOP_K / n_experts / 2 * 1.09) // 64) * 64


def _ffn_body(te_ref, tv_ref, x_ref, wg_ref, wu_ref, wd_ref, *rest, gated):
    del te_ref
    if gated:
        gate_ref, o_ref = rest
    else:
        (o_ref,) = rest
    i = pl.program_id(0)
    j = pl.program_id(1)
    valid = tv_ref[i]

    @pl.when(j == 0)
    def _():
        o_ref[...] = jnp.zeros_like(o_ref)

    def weights():
        return wg_ref[...].astype(BF16), wu_ref[...].astype(BF16), wd_ref[...].astype(BF16)

    def accumulate(rows, wg, wu, wd):
        xs = x_ref[rows, :]
        g = _dot(xs, wg)
        u = _dot(xs, wu)
        h = (g * jax.nn.sigmoid(g) * u).astype(BF16)
        o_ref[rows, :] += _dot(h, wd)

    tm = x_ref.shape[0]
    for stop in range(FFN_CHUNK, tm + FFN_CHUNK, FFN_CHUNK):
        @pl.when((valid > stop - FFN_CHUNK) & (valid <= stop))
        def _():
            accumulate(pl.ds(0, min(stop, tm)), *weights())

    if gated:
        @pl.when(j == pl.num_programs(1) - 1)
        def _():
            o_ref[...] = o_ref[...] * gate_ref[...]


def swiglu(x, w_gate, w_up, w_down, tile_expert, tile_valid, tm, row_gate=None, tf=256):
    rows, d = x.shape
    n_e, _, f = w_gate.shape
    n_tiles = rows // tm
    n_j = f // tf
    assert rows % tm == 0 and f % tf == 0 and tm % 16 == 0

    def jeff(i, j, tv):
        return jnp.where(tv[i] > 0, j, n_j - 1)

    once = pl.Buffered(1)
    in_specs = [
        pl.BlockSpec((tm, d), lambda i, j, te, tv: (i, 0), pipeline_mode=once),
        pl.BlockSpec((None, d, tf), lambda i, j, te, tv: (te[i], 0, jeff(i, j, tv))),
        pl.BlockSpec((None, d, tf), lambda i, j, te, tv: (te[i], 0, jeff(i, j, tv))),
        pl.BlockSpec((None, tf, d), lambda i, j, te, tv: (te[i], jeff(i, j, tv), 0)),
    ]
    args = [x, w_gate, w_up, w_down]
    if row_gate is not None:
        in_specs.append(pl.BlockSpec((tm, 1), lambda i, j, te, tv: (i, 0)))
        args.append(row_gate)
    return pl.pallas_call(
        functools.partial(_ffn_body, gated=row_gate is not None),
        grid_spec=pltpu.PrefetchScalarGridSpec(
            num_scalar_prefetch=2,
            grid=(n_tiles, n_j),
            in_specs=in_specs,
            out_specs=pl.BlockSpec((tm, d), lambda i, j, te, tv: (i, 0), pipeline_mode=once),
        ),
        out_shape=jax.ShapeDtypeStruct((rows, d), F32),
        compiler_params=_cparams(2, 62),
        name="swiglu",
    )(tile_expert, tile_valid, *args)


def _s5_tables(a_re, a_im, log_dt, b_re, b_im, c_re, c_im, n_cb):
    hp = lax.Precision.HIGHEST
    g, p = a_re.shape
    c = S5_GROUP_DIM
    gl = LANES // c
    L = S5_BLOCK
    dt = jnp.exp(log_dt)[:, None]
    mag = jnp.exp(a_re * dt)
    lb_re, lb_im = mag * jnp.cos(a_im * dt), mag * jnp.sin(a_im * dt)
    den = a_re * a_re + a_im * a_im
    q_re = ((lb_re - 1.0) * a_re + lb_im * a_im) / den
    q_im = (lb_im * a_re - (lb_re - 1.0) * a_im) / den
    bb_re = q_re[..., None] * b_re - q_im[..., None] * b_im
    bb_im = q_re[..., None] * b_im + q_im[..., None] * b_re
    pw_re, pw_im = [jnp.ones_like(lb_re)], [jnp.zeros_like(lb_re)]
    for _ in range(L):
        r, i = pw_re[-1], pw_im[-1]
        pw_re.append(r * lb_re - i * lb_im)
        pw_im.append(r * lb_im + i * lb_re)
    pw_re, pw_im = jnp.stack(pw_re), jnp.stack(pw_im)
    w_re = pw_re[:L, :, :, None] * bb_re[None] - pw_im[:L, :, :, None] * bb_im[None]
    w_im = pw_re[:L, :, :, None] * bb_im[None] + pw_im[:L, :, :, None] * bb_re[None]
    eye = jnp.eye(gl, dtype=F32)

    taps = (jnp.einsum("gcp,dgpk->dgkc", c_re, w_re, precision=hp)
            - jnp.einsum("gcp,dgpk->dgkc", c_im, w_im, precision=hp))
    krev = taps[::-1].reshape(L, n_cb, gl, c, c)
    krev = jnp.einsum("rjgkc,gh->jrgkhc", krev, eye).reshape(n_cb, L * LANES, LANES).astype(BF16)

    assert 2 * p == LANES
    wend = jnp.stack([w_re[::-1], w_im[::-1]], axis=0).reshape(2, L, n_cb, gl // 2, 2, p, c)
    wend = jnp.transpose(wend, (2, 0, 1, 3, 6, 4, 5)).reshape(n_cb, 2, L * LANES // 2, LANES)

    cmat = jnp.stack([c_re, -c_im], axis=0).reshape(2, n_cb, gl, c, p)
    cmat = jnp.einsum("rjgcp,gh->jrgphc", cmat, eye).reshape(n_cb, 2 * gl * p, LANES).astype(BF16)

    return krev, wend, cmat, pw_re.reshape(L + 1, g * p), pw_im.reshape(L + 1, g * p)


def _gather_blocks(u_ref, dst_ref, nb):
    for s in range(S5_BLOCK):
        dst_ref[:, s * LANES:(s + 1) * LANES] = u_ref[pl.ds(s, nb, stride=S5_BLOCK), :].astype(dst_ref.dtype)


def _s5_state_body(u_ref, w_ref, lr_ref, li_ref, h0_ref, cin_ref, fin_ref, ublk, whi, wlo, e_scr,
                   *, nb_prompt, nb):
    half = cin_ref.shape[1] // 2
    n_pair = half // LANES
    n_grp = LANES // S5_GROUP_DIM
    lane_par = lax.broadcasted_iota(jnp.int32, (S5_BLOCK, S5_GROUP_DIM, LANES), 2) // S5_STATE
    for r in range(2):
        w4 = w_ref[r].reshape(S5_BLOCK, n_pair, S5_GROUP_DIM, LANES)
        for hp in range(n_pair):
            pieces = [jnp.zeros((S5_BLOCK, S5_GROUP_DIM, LANES), F32)] * n_grp
            pieces[2 * hp] = jnp.where(lane_par == 0, w4[:, hp], 0.0)
            pieces[2 * hp + 1] = jnp.where(lane_par == 1, w4[:, hp], 0.0)
            blk = jnp.stack(pieces, axis=1).reshape(S5_BLOCK * LANES, LANES)
            b_hi, b_lo = _split_bf16(blk)
            cols = slice((r * n_pair + hp) * LANES, (r * n_pair + hp + 1) * LANES)
            whi[:, cols] = b_hi
            wlo[:, cols] = b_lo
    _gather_blocks(u_ref, ublk, nb)
    u_hi, u_lo = _split_bf16(ublk[...])
    e_scr[...] = _dot3(u_hi, u_lo, whi[...], wlo[...])
    lr, li = lr_ref[...], li_ref[...]

    def step(m, carry):
        cr, ci = carry
        cin_ref[pl.ds(m, 1), :half] = cr
        cin_ref[pl.ds(m, 1), half:] = ci
        er = e_scr[pl.ds(m, 1), :half]
        ei = e_scr[pl.ds(m, 1), half:]
        return er + lr * cr - li * ci, ei + lr * ci + li * cr

    zero = jnp.zeros((1, half), F32)
    cr, ci = lax.fori_loop(0, nb_prompt, step, (zero, zero))
    fin_ref[0:1, :half] = cr
    fin_ref[0:1, half:] = ci
    h0 = h0_ref[...]
    cin_ref[nb_prompt:nb, :] = h0
    hr, hi = h0[:, :half], h0[:, half:]
    es = e_scr[nb_prompt:nb, :]
    fin_ref[1:1 + nb - nb_prompt, :half] = es[:, :half] + lr * hr - li * hi
    fin_ref[1:1 + nb - nb_prompt, half:] = es[:, half:] + lr * hi + li * hr


def _gelu_tanh(x):
    return x * (0.5 * (1.0 + jnp.tanh(math.sqrt(2.0 / math.pi) * (x + 0.044715 * (x * x * x)))))


def _s5_out_body(u_ref, cin_ref, krev_ref, cmat_ref, pr_ref, pi_ref, d_ref, z_ref, ublk, yscr, *, nb):
    half = cin_ref.shape[1] // 2
    _gather_blocks(u_ref, ublk, nb)
    cr, ci = cin_ref[:, :half], cin_ref[:, half:]
    cmat = cmat_ref[...]
    for i in range(S5_BLOCK):
        pr, pi = pr_ref[i:i + 1, :], pi_ref[i:i + 1, :]
        carried = jnp.concatenate([pr * cr - pi * ci, pr * ci + pi * cr], axis=1).astype(BF16)
        y_i = (_dot(ublk[:, :(i + 1) * LANES], krev_ref[(S5_BLOCK - 1 - i) * LANES:, :])
               + _dot(carried, cmat))
        yscr[pl.ds(i, nb, stride=S5_BLOCK), :] = y_i
    y = yscr[...] + d_ref[...] * u_ref[...]
    z_ref[...] = _gelu_tanh(y).astype(z_ref.dtype)


def s5_mixer(u, h0_re, h0_im, n_prompt, prm):
    t, d = u.shape
    g, p = prm["s5_a_re"].shape
    n_cb = d // LANES
    gl = LANES // S5_GROUP_DIM
    L = S5_BLOCK
    nb, nb_prompt = t // L, n_prompt // L
    nbs = nb - nb_prompt
    sw = 2 * gl * p
    krev, wend, cmat, pw_re, pw_im = _s5_tables(
        prm["s5_a_re"], prm["s5_a_im"], prm["s5_log_dt"], prm["s5_b_re"], prm["s5_b_im"],
        prm["s5_c_re"], prm["s5_c_im"], n_cb)
    h0 = jnp.concatenate([h0_re.reshape(nbs, n_cb, gl * p), h0_im.reshape(nbs, n_cb, gl * p)],
                         axis=-1).reshape(nbs, n_cb * sw)
    kw = L * LANES
    ucol = pl.BlockSpec((t, LANES), lambda j: (0, j))
    cin, fin = pl.pallas_call(
        functools.partial(_s5_state_body, nb_prompt=nb_prompt, nb=nb),
        grid=(n_cb,),
        in_specs=[ucol,
                  pl.BlockSpec((None, 2, kw // 2, LANES), lambda j: (j, 0, 0, 0)),
                  pl.BlockSpec((1, sw // 2), lambda j: (0, j)),
                  pl.BlockSpec((1, sw // 2), lambda j: (0, j)),
                  pl.BlockSpec((nbs, sw), lambda j: (0, j))],
        out_specs=[pl.BlockSpec((nb, sw), lambda j: (0, j)),
                   pl.BlockSpec((1 + nbs, sw), lambda j: (0, j))],
        out_shape=[jax.ShapeDtypeStruct((nb, n_cb * sw), F32),
                   jax.ShapeDtypeStruct((1 + nbs, n_cb * sw), F32)],
        scratch_shapes=[pltpu.VMEM((nb, kw), F32), pltpu.VMEM((kw, sw), BF16), pltpu.VMEM((kw, sw), BF16),
                        pltpu.VMEM((nb, sw), F32)],
        compiler_params=_cparams(1, 56),
        name="s5_state",
    )(u, wend, pw_re[L:L + 1], pw_im[L:L + 1], h0)
    z = pl.pallas_call(
        functools.partial(_s5_out_body, nb=nb),
        grid=(n_cb,),
        in_specs=[ucol,
                  pl.BlockSpec((nb, sw), lambda j: (0, j)),
                  pl.BlockSpec((None, kw, LANES), lambda j: (j, 0, 0)),
                  pl.BlockSpec((None, sw, LANES), lambda j: (j, 0, 0)),
                  pl.BlockSpec((L, sw // 2), lambda j: (0, j)),
                  pl.BlockSpec((L, sw // 2), lambda j: (0, j)),
                  pl.BlockSpec((1, LANES), lambda j: (0, j))],
        out_specs=pl.BlockSpec((t, LANES), lambda j: (0, j)),
        out_shape=jax.ShapeDtypeStruct((t, d), BF16),
        scratch_shapes=[pltpu.VMEM((nb, kw), BF16), pltpu.VMEM((t, LANES), F32)],
        compiler_params=_cparams(1, 56),
        name="s5_out",
    )(u, cin, krev, cmat, pw_re[1:], pw_im[1:], prm["s5_d"].reshape(1, d))
    fin = fin.reshape(1 + nbs, n_cb, 2, gl * p)
    fin_re = fin[:, :, 0].reshape(1 + nbs, g, p)
    fin_im = fin[:, :, 1].reshape(1 + nbs, g, p)
    return z, fin_re[:1], fin_im[:1], fin_re[1:], fin_im[1:]


def _cumsum_body(x_ref, init_ref, o_ref, *, n_rows):
    def step(t, carry):
        carry = carry + x_ref[pl.ds(t, 1), :]
        o_ref[pl.ds(t, 1), :] = carry
        return carry

    lax.fori_loop(0, n_rows, step, init_ref[...], unroll=8)


def cumsum_rows(x, init):
    r, c = x.shape
    return pl.pallas_call(
        functools.partial(_cumsum_body, n_rows=r),
        out_shape=jax.ShapeDtypeStruct((r, c), F32),
        compiler_params=pltpu.CompilerParams(vmem_limit_bytes=40 << 20),
        name="cumsum_rows",
    )(x, init)


def _nt_dot(a, b):
    return lax.dot_general(a, b, (((1,), (1,)), ((), ())), preferred_element_type=F32)


def _online_softmax_update(h, s, v, m_scr, l_scr, acc_scr):
    m_prev = m_scr[h]
    m_new = jnp.maximum(m_prev, jnp.max(s, axis=-1, keepdims=True))
    alpha = jnp.exp2(m_prev - m_new)
    width = s.shape[1]
    m_wide = jnp.tile(m_new, (1, width // LANES)) if width >= LANES else m_new[:, :width]
    p = jnp.exp2(s - m_wide)
    l_scr[h] = alpha * l_scr[h] + jnp.sum(p, axis=-1, keepdims=True)
    acc_scr[h] = alpha * acc_scr[h] + _dot(p.astype(BF16), v)
    m_scr[h] = m_new


def _init_softmax_state(m_scr, l_scr, acc_scr):
    m_scr[...] = jnp.full_like(m_scr, NEG_INF)
    l_scr[...] = jnp.zeros_like(l_scr)
    acc_scr[...] = jnp.zeros_like(acc_scr)


def _causal(s):
    row = lax.broadcasted_iota(jnp.int32, s.shape, 0)
    col = lax.broadcasted_iota(jnp.int32, s.shape, 1)
    return jnp.where(col <= row, s, NEG_INF)


def _fox_prompt_body(q_ref, k_ref, v_ref, cq_ref, ck_ref, o_ref, qa_scr, m_scr, l_scr, acc_scr, *, hb):
    qi = pl.program_id(1)
    ki = pl.program_id(2)
    blk = q_ref.shape[0]

    @pl.when(ki == 0)
    def _():
        _init_softmax_state(m_scr, l_scr, acc_scr)
        lane = lax.broadcasted_iota(jnp.int32, (blk, HEAD_DIM), 1)
        for h in range(hb):
            c = jnp.broadcast_to(cq_ref[h], (blk, HEAD_DIM))
            c0 = c.astype(BF16).astype(F32)
            c1 = (c - c0).astype(BF16).astype(F32)
            c2 = (c - c0 - c1).astype(BF16).astype(F32)
            aug = jnp.where(lane == 0, c0, jnp.where(lane == 1, c1, jnp.where(lane == 2, c2, 0.0)))
            qa_scr[h, :, :HEAD_DIM] = q_ref[:, h * HEAD_DIM:(h + 1) * HEAD_DIM]
            qa_scr[h, :, HEAD_DIM:] = aug.astype(BF16)

    def process(masked):
        ones = jnp.ones((blk, HEAD_DIM), BF16)
        for h in range(hb):
            cols = slice(h * HEAD_DIM, (h + 1) * HEAD_DIM)
            k_aug = jnp.concatenate([k_ref[:, cols], ones], axis=1)
            s = _nt_dot(qa_scr[h], k_aug) - ck_ref[h]
            if masked:
                s = _causal(s)
            _online_softmax_update(h, s, v_ref[:, cols], m_scr, l_scr, acc_scr)

    @pl.when(ki < qi)
    def _():
        process(False)

    @pl.when(ki == qi)
    def _():
        process(True)

    @pl.when(ki == pl.num_programs(2) - 1)
    def _():
        for h in range(hb):
            o_ref[:, h * HEAD_DIM:(h + 1) * HEAD_DIM] = (acc_scr[h] / l_scr[h]).astype(o_ref.dtype)


def fox_prompt_attention(q, k, v, c, n_prompt, blk=1024, hb=4):
    d = q.shape[1]
    n_h = d // HEAD_DIM
    blk = _pick_tile(n_prompt, blk, LANES)
    hb = min(hb, n_h)
    nq = n_prompt // blk
    cq = c.T.reshape(n_h, n_prompt, 1)
    ck = c.T.reshape(n_h, 1, n_prompt)
    kv_spec = pl.BlockSpec((blk, hb * HEAD_DIM), lambda h, qi, ki: (jnp.minimum(ki, qi), h))
    return pl.pallas_call(
        functools.partial(_fox_prompt_body, hb=hb),
        grid=(n_h // hb, nq, nq),
        in_specs=[pl.BlockSpec((blk, hb * HEAD_DIM), lambda h, qi, ki: (qi, h)),
                  kv_spec, kv_spec,
                  pl.BlockSpec((hb, blk, 1), lambda h, qi, ki: (h, qi, 0)),
                  pl.BlockSpec((hb, 1, blk), lambda h, qi, ki: (h, 0, jnp.minimum(ki, qi)))],
        out_specs=pl.BlockSpec((blk, hb * HEAD_DIM), lambda h, qi, ki: (qi, h)),
        out_shape=jax.ShapeDtypeStruct((n_prompt, d), BF16),
        scratch_shapes=[pltpu.VMEM((hb, blk, 2 * HEAD_DIM), BF16),
                        pltpu.VMEM((hb, blk, LANES), F32), pltpu.VMEM((hb, blk, LANES), F32),
                        pltpu.VMEM((hb, blk, HEAD_DIM), F32)],
        compiler_params=_cparams(3, 48),
        name="fox_prompt",
    )(q, k, v, cq, ck)


def _fox_sample_body(q_ref, k_ref, v_ref, cq_ref, ckn_ref, ckp_ref, kc_hbm, vc_hbm, o_ref, kbuf, vbuf, sem, *, n_h):
    b = pl.program_id(0)
    h = pl.program_id(1)
    n = b * n_h + h
    slot = n % 2

    def cache_copies(bb, hh, sl):
        return (pltpu.make_async_copy(kc_hbm.at[bb, :, hh, :], kbuf.at[sl], sem.at[0, sl]),
                pltpu.make_async_copy(vc_hbm.at[bb, :, hh, :], vbuf.at[sl], sem.at[1, sl]))

    @pl.when(n == 0)
    def _():
        for cp in cache_copies(b, h, slot):
            cp.start()

    @pl.when(n + 1 < pl.num_programs(0) * n_h)
    def _():
        nxt = n + 1
        for cp in cache_copies(nxt // n_h, nxt % n_h, 1 - slot):
            cp.start()

    for cp in cache_copies(b, h, slot):
        cp.wait()

    q = q_ref[...]
    cq = cq_ref[...]
    s_past = _nt_dot(q, kbuf[slot].astype(BF16)) + cq - ckp_ref[...]
    s_new = _causal(_nt_dot(q, k_ref[...]) + cq - ckn_ref[...])
    m = jnp.maximum(jnp.max(s_past, axis=-1, keepdims=True), jnp.max(s_new, axis=-1, keepdims=True))
    p_past = jnp.exp2(s_past - m)
    p_new = jnp.exp2(s_new - m)
    l = jnp.sum(p_past, axis=-1, keepdims=True) + jnp.sum(p_new, axis=-1, keepdims=True)
    o = _dot(p_past.astype(BF16), vbuf[slot].astype(BF16)) + _dot(p_new.astype(BF16), v_ref[...])
    o_ref[...] = (o / l).astype(o_ref.dtype)


def fox_sample_attention(q, k, v, k_cache, v_cache, c_past, c_new, n_prompt):
    bsz, past, n_h, _ = k_cache.shape
    d = n_h * HEAD_DIM
    seq = c_new.shape[-1]
    row0 = n_prompt // seq
    new_spec = pl.BlockSpec((seq, HEAD_DIM), lambda b, h: (row0 + b, h))
    return pl.pallas_call(
        functools.partial(_fox_sample_body, n_h=n_h),
        grid=(bsz, n_h),
        in_specs=[new_spec, new_spec, new_spec,
                  pl.BlockSpec((None, None, seq, 1), lambda b, h: (b, h, 0, 0)),
                  pl.BlockSpec((None, None, 1, seq), lambda b, h: (b, h, 0, 0)),
                  pl.BlockSpec((None, None, 1, past), lambda b, h: (b, h, 0, 0)),
                  pl.BlockSpec(memory_space=pl.ANY), pl.BlockSpec(memory_space=pl.ANY)],
        out_specs=pl.BlockSpec((seq, HEAD_DIM), lambda b, h: (b, h)),
        out_shape=jax.ShapeDtypeStruct((bsz * seq, d), BF16),
        scratch_shapes=[pltpu.VMEM((2, past, HEAD_DIM), F32), pltpu.VMEM((2, past, HEAD_DIM), F32),
                        pltpu.SemaphoreType.DMA((2, 2))],
        compiler_params=_cparams(2, 32),
        name="fox_sample",
    )(q, k, v, c_new[..., None], c_new[:, :, None, :], c_past, k_cache, v_cache)


def _prefetched_rows(step_copies, n_rows):
    i = pl.program_id(0)
    slot = i % 2

    def start(step, sl):
        n = n_rows(step)

        def body(pair, c):
            for cp in step_copies(step, 2 * pair, sl):
                cp.start(priority=0)

            @pl.when(2 * pair + 1 < n)
            def _():
                for cp in step_copies(step, 2 * pair + 1, sl):
                    cp.start(priority=1)
            return c
        lax.fori_loop(0, (n + 1) // 2, body, 0)

    @pl.when(i == 0)
    def _():
        start(i, slot)

    @pl.when(i + 1 < pl.num_programs(0))
    def _():
        start(i + 1, 1 - slot)

    def wait(r, c):
        for cp in step_copies(i, r, slot):
            cp.wait()
        return c

    lax.fori_loop(0, n_rows(i), wait, 0)
    return slot


def _gather_body(idx_ref, nv_ref, x_hbm, o_ref, buf, sem):
    rows = o_ref.shape[0]

    @pl.when(pl.program_id(0) == 0)
    def _():
        buf[...] = jnp.zeros_like(buf)

    def step_copies(step, r, sl):
        src = x_hbm.at[pl.ds(idx_ref[step * rows + r], 1), :]
        return (pltpu.make_async_copy(src, buf.at[sl, pl.ds(r, 1), :], sem.at[sl]),)

    slot = _prefetched_rows(step_copies, lambda step: nv_ref[step])
    live = lax.broadcasted_iota(jnp.int32, (rows, 1), 0) < nv_ref[pl.program_id(0)]
    o_ref[...] = jnp.where(live, buf[slot], 0.0).astype(o_ref.dtype)


def gather_rows(x, idx, n_live, out_dtype, rows_per_step):
    n_out = idx.shape[0]
    d = x.shape[1]
    assert n_out % rows_per_step == 0 and rows_per_step % 16 == 0
    return pl.pallas_call(
        _gather_body,
        grid_spec=pltpu.PrefetchScalarGridSpec(
            num_scalar_prefetch=2,
            grid=(n_out // rows_per_step,),
            in_specs=[pl.BlockSpec(memory_space=pl.ANY)],
            out_specs=pl.BlockSpec((rows_per_step, d), lambda i, idx, nv: (i, 0)),
            scratch_shapes=[pltpu.VMEM((2, rows_per_step, d), F32), pltpu.SemaphoreType.DMA((2,))],
        ),
        out_shape=jax.ShapeDtypeStruct((n_out, d), out_dtype),
        compiler_params=_cparams(1, 32),
        name="gather_rows",
    )(idx, n_live, x)


COMBINE_ROWS = 128


def _combine_body(pos_ref, x_ref, g_ref, y_hbm, o_ref, buf, sem):
    def step_copies(step, r, sl):
        return tuple(
            pltpu.make_async_copy(y_hbm.at[pl.ds(pos_ref[2 * (step * COMBINE_ROWS + r) + k], 1), :],
                                  buf.at[sl, k, pl.ds(r, 1), :], sem.at[sl])
            for k in range(2))

    slot = _prefetched_rows(step_copies, lambda step: COMBINE_ROWS)
    o_ref[...] = _rms(x_ref[...] + (buf[slot, 0] + buf[slot, 1]), g_ref[...])


def combine_norm(x, y_sorted, pos, g):
    t, d = x.shape
    assert t % COMBINE_ROWS == 0
    return pl.pallas_call(
        _combine_body,
        grid_spec=pltpu.PrefetchScalarGridSpec(
            num_scalar_prefetch=1,
            grid=(t // COMBINE_ROWS,),
            in_specs=[pl.BlockSpec((COMBINE_ROWS, d), lambda i, pos: (i, 0)),
                      pl.BlockSpec((1, d), lambda i, pos: (0, 0)),
                      pl.BlockSpec(memory_space=pl.ANY)],
            out_specs=pl.BlockSpec((COMBINE_ROWS, d), lambda i, pos: (i, 0)),
            scratch_shapes=[pltpu.VMEM((2, 2, COMBINE_ROWS, d), F32), pltpu.SemaphoreType.DMA((2,))],
        ),
        out_shape=jax.ShapeDtypeStruct((t, d), F32),
        compiler_params=_cparams(1, 32),
        name="combine_norm",
    )(pos, x, g.reshape(1, d), y_sorted)


def _route(top_idx, top_gate, n_experts, tm, gather_rows_per_step):
    t = top_idx.shape[0]
    n_items = t * top_idx.shape[1]
    n_tiles = (n_items + n_experts * (tm - 1)) // tm
    flat_e = top_idx.reshape(-1)
    order = jnp.argsort(flat_e, stable=True)
    counts = jnp.zeros((n_experts,), jnp.int32).at[flat_e].add(1)
    tiles_e = (counts + tm - 1) // tm
    tile_start = jnp.cumsum(tiles_e) - tiles_e
    item_start = jnp.cumsum(counts) - counts
    e_sorted = flat_e[order]
    dest_sorted = tile_start[e_sorted] * tm + (jnp.arange(n_items, dtype=jnp.int32) - item_start[e_sorted])
    pos = jnp.zeros((n_items,), jnp.int32).at[order].set(dest_sorted)
    rows = n_tiles * tm
    row_token = jnp.zeros((rows,), jnp.int32).at[pos].set(jnp.arange(n_items, dtype=jnp.int32) // top_idx.shape[1])
    row_gate = jnp.zeros((rows,), F32).at[pos].set(top_gate.reshape(-1)).reshape(rows, 1)
    tile_ids = jnp.arange(n_tiles, dtype=jnp.int32)
    n_used = jnp.sum(tiles_e)
    tile_expert = jnp.sum(tile_ids[:, None] >= (tile_start + tiles_e)[None, :], axis=1).astype(jnp.int32)
    last_e = jnp.max(jnp.where(counts > 0, jnp.arange(n_experts), 0)).astype(jnp.int32)
    tile_expert = jnp.where(tile_ids < n_used, jnp.minimum(tile_expert, n_experts - 1), last_e)
    within = tile_ids - tile_start[tile_expert]
    tile_valid = jnp.where(tile_ids < n_used, jnp.clip(counts[tile_expert] - within * tm, 0, tm), 0).astype(jnp.int32)
    steps_per_tile = tm // gather_rows_per_step
    step_off = (jnp.arange(n_tiles * steps_per_tile, dtype=jnp.int32) % steps_per_tile) * gather_rows_per_step
    gather_valid = jnp.clip(jnp.repeat(tile_valid, steps_per_tile) - step_off, 0, gather_rows_per_step)
    return row_token, row_gate, pos, tile_expert, tile_valid, gather_valid.astype(jnp.int32)


def kernel(x_prompt, x_sample, state_s5_re, state_s5_im, cache_fox_k, cache_fox_v, cache_fox_logf, norm_mix0, s5_a_re, s5_a_im, s5_log_dt, s5_b_re, s5_b_im, s5_c_re, s5_c_im, s5_d, s5_w_glu_a, s5_w_glu_b, norm_ffn0, ffn_w_gate, ffn_w_up, ffn_w_down, norm_mix1, fox_w_in, fox_b_f, fox_w_o, norm_ffn1, moe_router, moe_w_gate, moe_w_up, moe_w_down, norm_final):
    bp, n_prompt, d = x_prompt.shape
    bs, seq, _ = x_sample.shape
    assert bp == 1 and seq == S5_BLOCK and n_prompt % S5_BLOCK == 0
    n_h = d // HEAD_DIM
    n_e = moe_router.shape[1]
    past = cache_fox_k.shape[1]
    x = jnp.concatenate([x_prompt.reshape(n_prompt, d), x_sample.reshape(bs * seq, d)], axis=0)
    t = x.shape[0]
    s5_prm = dict(s5_a_re=s5_a_re, s5_a_im=s5_a_im, s5_log_dt=s5_log_dt, s5_b_re=s5_b_re, s5_b_im=s5_b_im,
                  s5_c_re=s5_c_re, s5_c_im=s5_c_im, s5_d=s5_d)

    (u0,) = rmsnorm(x, [], norm_mix0, [F32])
    z, s5_re_p, s5_im_p, s5_re_s, s5_im_s = s5_mixer(u0, state_s5_re, state_s5_im, n_prompt, s5_prm)
    (xa,) = matmul(z, [s5_w_glu_a, s5_w_glu_b], 0, d, res=x, glu=True)
    (un0,) = rmsnorm(xa, [], norm_ffn0, [BF16])
    tm_dense = _pick_tile(t, 1152, 16)
    n_dense = t // tm_dense
    f0 = swiglu(un0, ffn_w_gate[None], ffn_w_up[None], ffn_w_down[None],
                jnp.zeros((n_dense,), jnp.int32), jnp.full((n_dense,), tm_dense, jnp.int32), tm_dense)

    x1, un1, logf_pad = rmsnorm_head(xa, [f0], norm_mix1, fox_w_in[:, 3 * d:], fox_b_f, BF16, "logf", True)
    logf = logf_pad[:, :n_h]
    scale = HEAD_DIM ** -0.5 * LOG2_E
    (q,) = matmul(un1, [fox_w_in], 0, d, scale=scale, out_kinds=("bf16",))
    k_f, k_b = matmul(un1, [fox_w_in], d, d, out_kinds=("f32", "bf16"))
    v_f, v_b = matmul(un1, [fox_w_in], 2 * d, d, out_kinds=("f32", "bf16"))

    c_prompt = cumsum_rows(logf[:n_prompt], jnp.zeros((1, n_h), F32))
    o_p = fox_prompt_attention(q, k_b, v_b, c_prompt * LOG2_E, n_prompt)
    lc = cache_fox_logf.astype(F32).transpose(1, 0, 2).reshape(past, bs * n_h)
    c_past = cumsum_rows(lc, jnp.zeros((1, bs * n_h), F32))
    ls = logf[n_prompt:].reshape(bs, seq, n_h).transpose(1, 0, 2).reshape(seq, bs * n_h)
    c_new = cumsum_rows(ls, c_past[past - 1:past])
    c_past_b = (c_past * LOG2_E).reshape(past, bs, n_h).transpose(1, 2, 0).reshape(bs, n_h, 1, past)
    c_new_b = (c_new * LOG2_E).reshape(seq, bs, n_h).transpose(1, 2, 0)
    o_s = fox_sample_attention(q, k_b, v_b, cache_fox_k, cache_fox_v, c_past_b, c_new_b, n_prompt)
    o = jnp.concatenate([o_p, o_s], axis=0)
    (xb,) = matmul(o, [fox_w_o], 0, d, res=x1)

    un2, route = rmsnorm_head(xb, [], norm_ffn1, moe_router, jnp.zeros((n_e,), F32), F32, "router", False)
    top_idx = route[:, :2].astype(jnp.int32)
    top_gate = route[:, 2:4]
    tm_moe = _moe_tile(t, n_e)
    row_token, row_gate, pos, tile_expert, tile_valid, gather_valid = _route(top_idx, top_gate, n_e, tm_moe, tm_moe // 4)
    x_sorted = gather_rows(un2, row_token, gather_valid, BF16, tm_moe // 4)
    y_sorted = swiglu(x_sorted, moe_w_gate, moe_w_up, moe_w_down, tile_expert, tile_valid, tm_moe, row_gate=row_gate)
    y = combine_norm(xb, y_sorted, pos, norm_final)

    k4 = lambda a, b: a.reshape(b, -1, n_h, HEAD_DIM)
    return (y[:n_prompt].reshape(bp, n_prompt, d), y[n_prompt:].reshape(bs, seq, d),
            s5_re_p, s5_im_p, s5_re_s, s5_im_s,
            k4(k_f[:n_prompt], bp), k4(v_f[:n_prompt], bp), logf[:n_prompt].reshape(bp, n_prompt, n_h),
            k4(k_f[n_prompt:], bs), k4(v_f[n_prompt:], bs), logf[n_prompt:].reshape(bs, seq, n_h))
```
